```python
import jax, jax.numpy as jnp
from jax import lax
import numpy as np

D_MODEL = 1024
BATCH = 2
SEQ = 8192
DEPTH = 2
DEC_BATCH = 8
DEC_SEQ = 8192
PAST_LEN = 128

N_MIXERS = 2
N_ATTN_LAYERS = (DEPTH + 1) // 2
N_POOL_LAYERS = DEPTH // 2
N_HEADS = 8
N_KV_HEADS = 2
HEAD_DIM = 128
GROUP = N_HEADS // N_KV_HEADS
Q_DIM = N_HEADS * HEAD_DIM
KV_DIM = N_KV_HEADS * HEAD_DIM
ATTN_IN = 2 * Q_DIM + 2 * KV_DIM
AXIS_DIM = HEAD_DIM // 2
ROPE_THETA = 10000.0
Q_BLOCK = 128
GRID_W = 64
POOL_WINDOWS = (2, 4, 8, 16)
N_POOL_GROUPS = len(POOL_WINDOWS)
POOL_WIDTH = D_MODEL
POOL_GROUP_WIDTH = POOL_WIDTH // N_POOL_GROUPS
RMS_EPS = 1e-6

kernel_name = "hybrid_axial_gqa_multiscale_pool_encoder"


def _rmsnorm(x, g):
    x32 = x.astype(jnp.float32)
    y = x32 * lax.rsqrt(jnp.mean(x32 * x32, axis=-1, keepdims=True) + RMS_EPS)
    return (y * g.astype(jnp.float32)).astype(x.dtype)


def _axial_rope(s):
    rows = s // GRID_W
    row = jnp.broadcast_to(jnp.arange(rows)[:, None], (rows, GRID_W)).reshape(s).astype(jnp.float32)
    col = jnp.broadcast_to(jnp.arange(GRID_W)[None, :], (rows, GRID_W)).reshape(s).astype(jnp.float32)
    inv = ROPE_THETA ** (-jnp.arange(0, AXIS_DIM, 2, dtype=jnp.float32) / AXIS_DIM)
    ang = jnp.concatenate([row[:, None] * inv, col[:, None] * inv], axis=-1)
    return jnp.cos(ang), jnp.sin(ang)


def _apply_rope(x, cos, sin):
    b, s, h, d = x.shape
    xr = x.reshape(b, s, h, d // 2, 2)
    x0, x1 = xr[..., 0], xr[..., 1]
    c = cos[None, :, None, :]
    sn = sin[None, :, None, :]
    return jnp.stack([x0 * c - x1 * sn, x0 * sn + x1 * c], axis=-1).reshape(b, s, h, d)


def _attention_mixer(h, w_in, q_g, k_g, w_out):
    b, s, _ = h.shape
    q, k, v, gate = jnp.split(h @ w_in, [Q_DIM, Q_DIM + KV_DIM, Q_DIM + 2 * KV_DIM], axis=-1)
    q = q.reshape(b, s, N_HEADS, HEAD_DIM)
    k = k.reshape(b, s, N_KV_HEADS, HEAD_DIM)
    v = v.reshape(b, s, N_KV_HEADS, HEAD_DIM)
    cos, sin = _axial_rope(s)
    q = (_apply_rope(_rmsnorm(q, q_g).astype(jnp.float32), cos, sin) * (HEAD_DIM ** -0.5)).astype(h.dtype)
    k = _apply_rope(_rmsnorm(k, k_g).astype(jnp.float32), cos, sin).astype(h.dtype)
    nb = s // Q_BLOCK
    qb = q.reshape(b, nb, Q_BLOCK, N_KV_HEADS, GROUP, HEAD_DIM).transpose(1, 0, 2, 3, 4, 5)

    def block(qblk):
        sc = jnp.einsum('bqkgd,bskd->bkgqs', qblk, k, preferred_element_type=jnp.float32)
        p = jax.nn.softmax(sc, axis=-1)
        return jnp.einsum('bkgqs,bskd->bqkgd', p.astype(v.dtype), v)

    o = lax.map(block, qb)
    o = o.transpose(1, 0, 2, 3, 4, 5).reshape(b, s, Q_DIM)
    return (o * jax.nn.silu(gate)) @ w_out


def _pool_mixer(h, w_in, w_group, scale, w_out):
    b, s, _ = h.shape
    u, gate = jnp.split(h @ w_in, 2, axis=-1)
    u32 = u.astype(jnp.float32)
    cs = jnp.concatenate([jnp.zeros((b, 1, POOL_WIDTH), jnp.float32), jnp.cumsum(u32, axis=1)], axis=1)
    t = jnp.arange(s)
    pooled = []
    for j, w in enumerate(POOL_WINDOWS):
        lo = jnp.clip(t - w // 2, 0, s)
        hi = jnp.clip(t - w // 2 + w, 0, s)
        csg = cs[..., j * POOL_GROUP_WIDTH:(j + 1) * POOL_GROUP_WIDTH]
        cnt = (hi - lo).astype(jnp.float32)[None, :, None]
        pooled.append((jnp.take(csg, hi, axis=1) - jnp.take(csg, lo, axis=1)) / cnt)
    mix = (jnp.concatenate(pooled, axis=-1) - u32).astype(h.dtype)
    mix = jnp.einsum('bsgc,gcd->bsgd', mix.reshape(b, s, N_POOL_GROUPS, POOL_GROUP_WIDTH), w_group)
    mix = mix.reshape(b, s, POOL_WIDTH) * scale
    return (mix * jax.nn.silu(gate)) @ w_out


def _trunk(x, c, norm_g, ada_w, ada_b, attn_w_in, attn_q_norm, attn_k_norm, attn_w_out,
           pool_w_in, pool_w_group, pool_scale, pool_w_out):
    cs = jax.nn.silu(c)
    for i in range(DEPTH):
        mod = cs @ ada_w[i] + ada_b[i]
        shift, scl, gate = jnp.split(mod, 3, axis=-1)
        h = _rmsnorm(x, norm_g[i]) * (1 + scl[:, None, :]) + shift[:, None, :]
        j = i // N_MIXERS
        if i % N_MIXERS == 0:
            out = _attention_mixer(h, attn_w_in[j], attn_q_norm[j], attn_k_norm[j], attn_w_out[j])
        else:
            out = _pool_mixer(h, pool_w_in[j], pool_w_group[j], pool_scale[j], pool_w_out[j])
        x = x + gate[:, None, :] * out
    return x


def setup_inputs(seed: int = 0) -> dict:
    key = jax.random.key(seed)
    ks = jax.random.split(key, 16)
    f32 = jnp.float32
    nrm = lambda k, shp: jax.random.normal(k, shp, f32)
    return {
        "x_prompt": nrm(ks[0], (BATCH, SEQ, D_MODEL)),
        "x_sample": nrm(ks[1], (DEC_BATCH, DEC_SEQ, D_MODEL)),
        "c_prompt": nrm(ks[2], (BATCH, D_MODEL)),
        "c_sample": nrm(ks[3], (DEC_BATCH, D_MODEL)),
        "norm_g": 1.0 + 0.02 * nrm(ks[4], (DEPTH, D_MODEL)),
        "ada_w": nrm(ks[5], (DEPTH, D_MODEL, 3 * D_MODEL)) * (0.5 * D_MODEL ** -0.5),
        "ada_b": 0.01 * nrm(ks[6], (DEPTH, 3 * D_MODEL)),
        "attn_w_in": nrm(ks[7], (N_ATTN_LAYERS, D_MODEL, ATTN_IN)) * D_MODEL ** -0.5,
        "attn_q_norm": 1.0 + 0.02 * nrm(ks[8], (N_ATTN_LAYERS, HEAD_DIM)),
        "attn_k_norm": 1.0 + 0.02 * nrm(ks[9], (N_ATTN_LAYERS, HEAD_DIM)),
        "attn_w_out": nrm(ks[10], (N_ATTN_LAYERS, Q_DIM, D_MODEL)) * Q_DIM ** -0.5,
        "pool_w_in": nrm(ks[11], (N_POOL_LAYERS, D_MODEL, 2 * POOL_WIDTH)) * D_MODEL ** -0.5,
        "pool_w_group": nrm(ks[12], (N_POOL_LAYERS, N_POOL_GROUPS, POOL_GROUP_WIDTH, POOL_GROUP_WIDTH)) * POOL_GROUP_WIDTH ** -0.5,
        "pool_scale": 1.0 + 0.1 * nrm(ks[13], (N_POOL_LAYERS, POOL_WIDTH)),
        "pool_w_out": nrm(ks[14], (N_POOL_LAYERS, POOL_WIDTH, D_MODEL)) * POOL_WIDTH ** -0.5,
    }


def reference(x_prompt, x_sample, c_prompt, c_sample, norm_g, ada_w, ada_b, attn_w_in,
              attn_q_norm, attn_k_norm, attn_w_out, pool_w_in, pool_w_group, pool_scale, pool_w_out):
    y_prompt = _trunk(x_prompt, c_prompt, norm_g, ada_w, ada_b, attn_w_in, attn_q_norm, attn_k_norm,
                      attn_w_out, pool_w_in, pool_w_group, pool_scale, pool_w_out)
    y_sample = _trunk(x_sample, c_sample, norm_g, ada_w, ada_b, attn_w_in, attn_q_norm, attn_k_norm,
                      attn_w_out, pool_w_in, pool_w_group, pool_scale, pool_w_out)
    return (y_prompt, y_sample)
```

```python
import functools

import jax
import jax.numpy as jnp
from jax import lax
from jax.experimental import pallas as pl
from jax.experimental.pallas import tpu as pltpu

D_MODEL = 1024
DEPTH = 2
N_HEADS = 8
N_KV_HEADS = 2
HEAD_DIM = 128
GROUP = N_HEADS // N_KV_HEADS
Q_DIM = N_HEADS * HEAD_DIM
KV_DIM = N_KV_HEADS * HEAD_DIM
ATTN_IN = 2 * Q_DIM + 2 * KV_DIM
AXIS_DIM = HEAD_DIM // 2
ROPE_THETA = 10000.0
GRID_W = 64
POOL_WINDOWS = (2, 4, 8, 16)
N_POOL_GROUPS = len(POOL_WINDOWS)
POOL_WIDTH = D_MODEL
POOL_GROUP_WIDTH = POOL_WIDTH // N_POOL_GROUPS
RMS_EPS = 1e-6

V7X_SUBLANES = 8
V7X_LANES = 128
V7X_VMEM_BYTES = 64 * 1024 * 1024
VMEM_LIMIT_BYTES = (V7X_VMEM_BYTES * 3) // 4

ROW_TILE = 512
Q_TILE = 512
KV_TILE = 512
POOL_HALO = max(POOL_WINDOWS) // 2

BF16 = jnp.bfloat16
F32 = jnp.float32


def _compiler_params(n_grid_axes):
    return pltpu.CompilerParams(
        dimension_semantics=("arbitrary",) * n_grid_axes,
        vmem_limit_bytes=VMEM_LIMIT_BYTES,
    )


def _silu(x):
    return x * (1.0 / (1.0 + jnp.exp(-x)))


def _modulated_rmsnorm(x, norm_g, scl, shift):
    ms = jnp.mean(x * x, axis=-1, keepdims=True)
    return (x * lax.rsqrt(ms + RMS_EPS)) * norm_g * (1.0 + scl) + shift


def _mod_kernel(c_ref, w_ref, b_ref, o_ref):
    cs = _silu(c_ref[...])
    o_ref[0, 0] = (
        jnp.dot(cs, w_ref[0], precision=lax.Precision.HIGHEST, preferred_element_type=F32)
        + b_ref[0, 0]
    )


def _modulation(c_pad, ada_w, ada_b):
    bp = c_pad.shape[0]
    return pl.pallas_call(
        _mod_kernel,
        grid=(DEPTH, 3),
        in_specs=[
            pl.BlockSpec((bp, D_MODEL), lambda i, j: (0, 0)),
            pl.BlockSpec((1, D_MODEL, D_MODEL), lambda i, j: (i, 0, j)),
            pl.BlockSpec((1, 1, 1, D_MODEL), lambda i, j: (i, j, 0, 0)),
        ],
        out_specs=pl.BlockSpec((1, 1, bp, D_MODEL), lambda i, j: (i, j, 0, 0)),
        out_shape=jax.ShapeDtypeStruct((DEPTH, 3, bp, D_MODEL), F32),
        compiler_params=_compiler_params(2),
        name="adaln_modulation",
    )(c_pad, ada_w, ada_b.reshape(DEPTH, 3, 1, D_MODEL))


def _attn_in_kernel(x_ref, mod_ref, ng_ref, w_ref, gq_ref, gk_ref, cos_ref, sin_ref,
                    q_ref, k_ref, v_ref, sg_ref):
    h = _modulated_rmsnorm(x_ref[0], ng_ref[...], mod_ref[1, 0], mod_ref[0, 0])
    y = jnp.dot(h.astype(BF16), w_ref[...], preferred_element_type=F32)
    cos = cos_ref[...]
    sin = sin_ref[...]

    def norm_rope(t, g):
        ms = jnp.mean(t * t, axis=-1, keepdims=True)
        tn = t * lax.rsqrt(ms + RMS_EPS) * g
        return tn * cos + pltpu.roll(tn, HEAD_DIM // 2, 1) * sin

    for hh in range(N_HEADS):
        q_ref[0, hh] = norm_rope(y[:, hh * HEAD_DIM:(hh + 1) * HEAD_DIM], gq_ref[...]).astype(BF16)
    for j in range(N_KV_HEADS):
        lo = Q_DIM + j * HEAD_DIM
        k_ref[0, j] = norm_rope(y[:, lo:lo + HEAD_DIM], gk_ref[...]).astype(BF16)
        lo = Q_DIM + KV_DIM + j * HEAD_DIM
        v_ref[0, j] = y[:, lo:lo + HEAD_DIM].astype(BF16)
    sg_ref[0] = _silu(y[:, Q_DIM + 2 * KV_DIM:]).astype(BF16)


def _attn_in_proj(x, mod, norm_g, w_in, gq, gk, cos_t, sin_t):
    b, s, _ = x.shape
    tm = ROW_TILE
    const = lambda *_: (0, 0)
    return pl.pallas_call(
        _attn_in_kernel,
        grid=(b, s // tm),
        in_specs=[
            pl.BlockSpec((1, tm, D_MODEL), lambda bi, i: (bi, i, 0)),
            pl.BlockSpec((3, 1, 1, D_MODEL), lambda bi, i: (0, bi, 0, 0)),
            pl.BlockSpec((1, D_MODEL), const),
            pl.BlockSpec((D_MODEL, ATTN_IN), const),
            pl.BlockSpec((1, HEAD_DIM), const),
            pl.BlockSpec((1, HEAD_DIM), const),
            pl.BlockSpec((tm, HEAD_DIM), lambda bi, i: (i, 0)),
            pl.BlockSpec((tm, HEAD_DIM), lambda bi, i: (i, 0)),
        ],
        out_specs=[
            pl.BlockSpec((1, N_HEADS, tm, HEAD_DIM), lambda bi, i: (bi, 0, i, 0)),
            pl.BlockSpec((1, N_KV_HEADS, tm, HEAD_DIM), lambda bi, i: (bi, 0, i, 0)),
            pl.BlockSpec((1, N_KV_HEADS, tm, HEAD_DIM), lambda bi, i: (bi, 0, i, 0)),
            pl.BlockSpec((1, tm, Q_DIM), lambda bi, i: (bi, i, 0)),
        ],
        out_shape=[
            jax.ShapeDtypeStruct((b, N_HEADS, s, HEAD_DIM), BF16),
            jax.ShapeDtypeStruct((b, N_KV_HEADS, s, HEAD_DIM), BF16),
            jax.ShapeDtypeStruct((b, N_KV_HEADS, s, HEAD_DIM), BF16),
            jax.ShapeDtypeStruct((b, s, Q_DIM), BF16),
        ],
        compiler_params=_compiler_params(2),
        name="attn_in_proj",
    )(x, mod, norm_g, w_in, gq, gk, cos_t, sin_t)


def _flash_kernel(q_ref, k_ref, v_ref, o_ref, m_sc, l_sc, acc_sc, *, n_kv, tk):
    q = q_ref[0, 0]
    m_sc[...] = jnp.full(m_sc.shape, -jnp.inf, F32)
    l_sc[...] = jnp.zeros(l_sc.shape, F32)
    acc_sc[...] = jnp.zeros(acc_sc.shape, F32)

    def body(j, carry):
        start = pl.multiple_of(j * tk, tk)
        ks = k_ref[0, 0, pl.ds(start, tk), :]
        vs = v_ref[0, 0, pl.ds(start, tk), :]
        s = lax.dot_general(q, ks, (((1,), (1,)), ((), ())), preferred_element_type=F32)
        m_prev = m_sc[...]
        m_next = jnp.maximum(m_prev, jnp.max(s, axis=1, keepdims=True))
        alpha = jnp.exp(m_prev - m_next)
        p = jnp.exp(s - pltpu.repeat(m_next, tk // V7X_LANES, 1))
        l_sc[...] = alpha * l_sc[...] + jnp.sum(p, axis=1, keepdims=True)
        acc_sc[...] = acc_sc[...] * alpha + jnp.dot(p.astype(BF16), vs, preferred_element_type=F32)
        m_sc[...] = m_next
        return carry

    lax.fori_loop(0, n_kv, body, 0)
    o_ref[0, 0] = (acc_sc[...] / l_sc[...]).astype(BF16)


def _flash_attention(q, k, v):
    b, nkv, sq, dh = q.shape
    s = k.shape[2]
    tq, tk = Q_TILE, KV_TILE
    kern = functools.partial(_flash_kernel, n_kv=s // tk, tk=tk)
    return pl.pallas_call(
        kern,
        grid=(b, nkv, sq // tq),
        in_specs=[
            pl.BlockSpec((1, 1, tq, dh), lambda bi, g, i: (bi, g, i, 0)),
            pl.BlockSpec((1, 1, s, dh), lambda bi, g, i: (bi, g, 0, 0)),
            pl.BlockSpec((1, 1, s, dh), lambda bi, g, i: (bi, g, 0, 0)),
        ],
        out_specs=pl.BlockSpec((1, 1, tq, dh), lambda bi, g, i: (bi, g, i, 0)),
        out_shape=jax.ShapeDtypeStruct((b, nkv, sq, dh), BF16),
        scratch_shapes=[
            pltpu.VMEM((tq, V7X_LANES), F32),
            pltpu.VMEM((tq, V7X_LANES), F32),
            pltpu.VMEM((tq, dh), F32),
        ],
        compiler_params=_compiler_params(3),
        name="flash_attention",
    )(q, k, v)


def _attn_out_kernel(o_ref, sg_ref, x_ref, mod_ref, w_ref, y_ref):
    o = jnp.concatenate([o_ref[0, hh] for hh in range(N_HEADS)], axis=1)
    out = jnp.dot(o * sg_ref[0], w_ref[...], preferred_element_type=F32)
    y_ref[0] = x_ref[0] + mod_ref[2, 0] * out


def _attn_out_proj(o, sg, x, mod, w_out):
    b, s, _ = x.shape
    tm = ROW_TILE
    return pl.pallas_call(
        _attn_out_kernel,
        grid=(b, s // tm),
        in_specs=[
            pl.BlockSpec((1, N_HEADS, tm, HEAD_DIM), lambda bi, i: (bi, 0, i, 0)),
            pl.BlockSpec((1, tm, Q_DIM), lambda bi, i: (bi, i, 0)),
            pl.BlockSpec((1, tm, D_MODEL), lambda bi, i: (bi, i, 0)),
            pl.BlockSpec((3, 1, 1, D_MODEL), lambda bi, i: (0, bi, 0, 0)),
            pl.BlockSpec((Q_DIM, D_MODEL), lambda bi, i: (0, 0)),
        ],
        out_specs=pl.BlockSpec((1, tm, D_MODEL), lambda bi, i: (bi, i, 0)),
        out_shape=jax.ShapeDtypeStruct((b, s, D_MODEL), F32),
        compiler_params=_compiler_params(2),
        name="attn_out_proj",
    )(o, sg, x, mod, w_out)


def _pool_kernel(xp_ref, x_ref, xn_ref, mod_ref, ng_ref, win_ref, wg_ref, sc_ref, wout_ref, y_ref,
                 *, tm, seq):
    halo = POOL_HALO
    rows = tm + 2 * halo
    i = pl.program_id(1)
    x = x_ref[0]
    xx = jnp.concatenate([xp_ref[0], x, xn_ref[0]], axis=0)
    h = _modulated_rmsnorm(xx, ng_ref[...], mod_ref[1, 0], mod_ref[0, 0])
    y = jnp.dot(h.astype(BF16), win_ref[...], preferred_element_type=F32)
    t_all = lax.broadcasted_iota(jnp.int32, (rows, 1), 0) + (i * tm - halo)
    u = jnp.where((t_all >= 0) & (t_all < seq), y[:, :POOL_WIDTH], 0.0)
    gate = y[halo:halo + tm, POOL_WIDTH:]
    t = t_all[halo:halo + tm]

    def shifted(a, d):
        return pltpu.roll(a, (-d) % rows, 0)

    mixed = []
    for j, w in enumerate(POOL_WINDOWS):
        ug = u[:, j * POOL_GROUP_WIDTH:(j + 1) * POOL_GROUP_WIDTH]
        win = ug + shifted(ug, -1)
        span = 2
        while span < w:
            win = shifted(win, -(span // 2)) + shifted(win, span // 2)
            span *= 2
        cnt = jnp.minimum(t + w // 2, seq) - jnp.maximum(t - w // 2, 0)
        pooled = win[halo:halo + tm] * (1.0 / cnt.astype(F32))
        mix = (pooled - ug[halo:halo + tm]).astype(BF16)
        mixed.append(jnp.dot(mix, wg_ref[j], preferred_element_type=F32))
    z = jnp.concatenate(mixed, axis=1) * sc_ref[...] * _silu(gate)
    out = jnp.dot(z.astype(BF16), wout_ref[...], preferred_element_type=F32)
    y_ref[0] = x + mod_ref[2, 0] * out


def _pool_layer(x, mod, norm_g, w_in, w_group, scale, w_out):
    b, s, _ = x.shape
    tm = ROW_TILE
    halo = POOL_HALO
    per = tm // halo
    last = s // halo - 1
    const2 = lambda *_: (0, 0)
    kern = functools.partial(_pool_kernel, tm=tm, seq=s)
    return pl.pallas_call(
        kern,
        grid=(b, s // tm),
        in_specs=[
            pl.BlockSpec((1, halo, D_MODEL), lambda bi, i: (bi, jnp.maximum(i * per - 1, 0), 0)),
            pl.BlockSpec((1, tm, D_MODEL), lambda bi, i: (bi, i, 0)),
            pl.BlockSpec((1, halo, D_MODEL), lambda bi, i: (bi, jnp.minimum((i + 1) * per, last), 0)),
            pl.BlockSpec((3, 1, 1, D_MODEL), lambda bi, i: (0, bi, 0, 0)),
            pl.BlockSpec((1, D_MODEL), const2),
            pl.BlockSpec((D_MODEL, 2 * POOL_WIDTH), const2),
            pl.BlockSpec((N_POOL_GROUPS, POOL_GROUP_WIDTH, POOL_GROUP_WIDTH), lambda *_: (0, 0, 0)),
            pl.BlockSpec((1, POOL_WIDTH), const2),
            pl.BlockSpec((POOL_WIDTH, D_MODEL), const2),
        ],
        out_specs=pl.BlockSpec((1, tm, D_MODEL), lambda bi, i: (bi, i, 0)),
        out_shape=jax.ShapeDtypeStruct((b, s, D_MODEL), F32),
        compiler_params=_compiler_params(2),
        name="pool_layer",
    )(x, x, x, mod, norm_g, w_in, w_group, scale, w_out)


def _rope_tables(s):
    rows = s // GRID_W
    row = jnp.broadcast_to(jnp.arange(rows)[:, None], (rows, GRID_W)).reshape(s).astype(F32)
    col = jnp.broadcast_to(jnp.arange(GRID_W)[None, :], (rows, GRID_W)).reshape(s).astype(F32)
    inv = ROPE_THETA ** (-jnp.arange(0, AXIS_DIM, 2, dtype=F32) / AXIS_DIM)
    ang = jnp.concatenate([row[:, None] * inv, col[:, None] * inv], axis=-1)
    c, sn = jnp.cos(ang), jnp.sin(ang)
    return jnp.concatenate([c, c], axis=-1), jnp.concatenate([-sn, sn], axis=-1)


def _deinterleave(a, n_heads):
    lead = a.shape[:-1]
    a = a.reshape(lead + (n_heads, HEAD_DIM // 2, 2))
    a = jnp.concatenate([a[..., 0], a[..., 1]], axis=-1)
    return a.reshape(lead + (n_heads * HEAD_DIM,))


def _trunk(x, mod, params):
    (norm_g, attn_w_in, gq, gk, attn_w_out, cos_t, sin_t,
     pool_w_in, pool_w_group, pool_scale, pool_w_out) = params
    b, s, _ = x.shape
    q, k, v, sg = _attn_in_proj(x, mod[0], norm_g[0:1], attn_w_in, gq, gk, cos_t, sin_t)
    o = _flash_attention(q.reshape(b, N_KV_HEADS, GROUP * s, HEAD_DIM), k, v)
    x = _attn_out_proj(o.reshape(b, N_HEADS, s, HEAD_DIM), sg, x, mod[0], attn_w_out)
    return _pool_layer(x, mod[1], norm_g[1:2], pool_w_in, pool_w_group, pool_scale, pool_w_out)


def kernel(x_prompt, x_sample, c_prompt, c_sample, norm_g, ada_w, ada_b, attn_w_in, attn_q_norm,
           attn_k_norm, attn_w_out, pool_w_in, pool_w_group, pool_scale, pool_w_out):
    assert DEPTH == 2 and attn_w_in.shape[0] == 1 and pool_w_in.shape[0] == 1
    bp, bs = x_prompt.shape[0], x_sample.shape[0]
    s = x_prompt.shape[1]
    assert x_sample.shape[1] == s and s % ROW_TILE == 0 and s % KV_TILE == 0

    c_all = jnp.concatenate([c_prompt, c_sample], axis=0)
    pad = (-c_all.shape[0]) % V7X_SUBLANES
    c_pad = jnp.pad(c_all, ((0, pad), (0, 0)))
    mod = _modulation(c_pad, ada_w, ada_b)
    mod = mod.reshape(DEPTH, 3, c_pad.shape[0], 1, D_MODEL)

    w_in = attn_w_in[0]
    w_in = jnp.concatenate(
        [_deinterleave(w_in[:, :Q_DIM], N_HEADS),
         _deinterleave(w_in[:, Q_DIM:Q_DIM + KV_DIM], N_KV_HEADS),
         w_in[:, Q_DIM + KV_DIM:]], axis=1).astype(BF16)
    gq = (_deinterleave(attn_q_norm[0], 1) * (HEAD_DIM ** -0.5)).reshape(1, HEAD_DIM)
    gk = _deinterleave(attn_k_norm[0], 1).reshape(1, HEAD_DIM)
    cos_t, sin_t = _rope_tables(s)
    params = (norm_g, w_in, gq, gk, attn_w_out[0].astype(BF16), cos_t, sin_t,
              pool_w_in[0].astype(BF16), pool_w_group[0].astype(BF16),
              pool_scale[0].reshape(1, POOL_WIDTH), pool_w_out[0].astype(BF16))

    y_prompt = _trunk(x_prompt, mod[:, :, :bp], params)
    y_sample = _trunk(x_sample, mod[:, :, bp:bp + bs], params)
    return (y_prompt, y_sample)
```

```python
import functools

import jax
import jax.numpy as jnp
from jax import lax
from jax.experimental import pallas as pl
from jax.experimental.pallas import tpu as pltpu

D_MODEL = 1024
DEPTH = 2
N_HEADS = 8
N_KV_HEADS = 2
HEAD_DIM = 128
GROUP = N_HEADS // N_KV_HEADS
Q_DIM = N_HEADS * HEAD_DIM
KV_DIM = N_KV_HEADS * HEAD_DIM
ATTN_IN = 2 * Q_DIM + 2 * KV_DIM
AXIS_DIM = HEAD_DIM // 2
ROPE_THETA = 10000.0
GRID_W = 64
POOL_WINDOWS = (2, 4, 8, 16)
N_POOL_GROUPS = len(POOL_WINDOWS)
POOL_WIDTH = D_MODEL
POOL_GROUP_WIDTH = POOL_WIDTH // N_POOL_GROUPS
RMS_EPS = 1e-6
LOG2_E = 1.4426950408889634
MAX_UNSHIFTED_LOGIT = 40.0

V7X_SUBLANES = 8
V7X_LANES = 128
V7X_VMEM_BYTES = 64 * 1024 * 1024
VMEM_LIMIT_BYTES = (V7X_VMEM_BYTES * 3) // 4

ROW_TILE = 512
Q_TILE = 512
KV_TILE = 512
KV_UNROLL = 16
POOL_HALO = max(POOL_WINDOWS) // 2

BF16 = jnp.bfloat16
F32 = jnp.float32


def _compiler_params(n_grid_axes):
    return pltpu.CompilerParams(
        dimension_semantics=("arbitrary",) * n_grid_axes,
        vmem_limit_bytes=VMEM_LIMIT_BYTES,
    )


def _silu(x):
    return x * (1.0 / (1.0 + jnp.exp(-x)))


def _modulated_rmsnorm(x, norm_g, scl, shift):
    ms = jnp.mean(x * x, axis=-1, keepdims=True)
    return (x * lax.rsqrt(ms + RMS_EPS)) * norm_g * (1.0 + scl) + shift


def _mod_kernel(c_ref, w_ref, b_ref, o_ref):
    cs = _silu(c_ref[...])
    o_ref[0, 0] = (
        jnp.dot(cs, w_ref[0], precision=lax.Precision.HIGHEST, preferred_element_type=F32)
        + b_ref[0, 0]
    )


def _modulation(c_pad, ada_w, ada_b):
    bp = c_pad.shape[0]
    return pl.pallas_call(
        _mod_kernel,
        grid=(DEPTH, 3),
        in_specs=[
            pl.BlockSpec((bp, D_MODEL), lambda i, j: (0, 0)),
            pl.BlockSpec((1, D_MODEL, D_MODEL), lambda i, j: (i, 0, j)),
            pl.BlockSpec((1, 1, 1, D_MODEL), lambda i, j: (i, j, 0, 0)),
        ],
        out_specs=pl.BlockSpec((1, 1, bp, D_MODEL), lambda i, j: (i, j, 0, 0)),
        out_shape=jax.ShapeDtypeStruct((DEPTH, 3, bp, D_MODEL), F32),
        compiler_params=_compiler_params(2),
        name="adaln_modulation",
    )(c_pad, ada_w, ada_b.reshape(DEPTH, 3, 1, D_MODEL))


def _attn_in_kernel(x_ref, mod_ref, ng_ref, w_ref, gq_ref, gk_ref, cos_ref, sin_ref,
                    q_ref, k_ref, v_ref, sg_ref):
    h = _modulated_rmsnorm(x_ref[0], ng_ref[...], mod_ref[1, 0], mod_ref[0, 0])
    y = jnp.dot(h.astype(BF16), w_ref[...], preferred_element_type=F32)
    cos = cos_ref[...]
    sin = sin_ref[...]

    def norm_rope(t, g):
        ms = jnp.mean(t * t, axis=-1, keepdims=True)
        tn = t * lax.rsqrt(ms + RMS_EPS) * g
        return tn * cos + pltpu.roll(tn, HEAD_DIM // 2, 1) * sin

    for hh in range(N_HEADS):
        q_ref[0, hh] = norm_rope(y[:, hh * HEAD_DIM:(hh + 1) * HEAD_DIM], gq_ref[...]).astype(BF16)
    for j in range(N_KV_HEADS):
        lo = Q_DIM + j * HEAD_DIM
        k_ref[0, j] = norm_rope(y[:, lo:lo + HEAD_DIM], gk_ref[...]).astype(BF16)
        lo = Q_DIM + KV_DIM + j * HEAD_DIM
        v_ref[0, j] = y[:, lo:lo + HEAD_DIM].astype(BF16)
    sg_ref[0] = _silu(y[:, Q_DIM + 2 * KV_DIM:]).astype(BF16)


def _attn_in_proj(x, mod, norm_g, w_in, gq, gk, cos_t, sin_t):
    b, s, _ = x.shape
    tm = ROW_TILE
    const = lambda *_: (0, 0)
    return pl.pallas_call(
        _attn_in_kernel,
        grid=(b, s // tm),
        in_specs=[
            pl.BlockSpec((1, tm, D_MODEL), lambda bi, i: (bi, i, 0)),
            pl.BlockSpec((3, 1, 1, D_MODEL), lambda bi, i: (0, bi, 0, 0)),
            pl.BlockSpec((1, D_MODEL), const),
            pl.BlockSpec((D_MODEL, ATTN_IN), const),
            pl.BlockSpec((1, HEAD_DIM), const),
            pl.BlockSpec((1, HEAD_DIM), const),
            pl.BlockSpec((tm, HEAD_DIM), lambda bi, i: (i, 0)),
            pl.BlockSpec((tm, HEAD_DIM), lambda bi, i: (i, 0)),
        ],
        out_specs=[
            pl.BlockSpec((1, N_HEADS, tm, HEAD_DIM), lambda bi, i: (bi, 0, i, 0)),
            pl.BlockSpec((1, N_KV_HEADS, tm, HEAD_DIM), lambda bi, i: (bi, 0, i, 0)),
            pl.BlockSpec((1, N_KV_HEADS, tm, HEAD_DIM), lambda bi, i: (bi, 0, i, 0)),
            pl.BlockSpec((1, tm, Q_DIM), lambda bi, i: (bi, i, 0)),
        ],
        out_shape=[
            jax.ShapeDtypeStruct((b, N_HEADS, s, HEAD_DIM), BF16),
            jax.ShapeDtypeStruct((b, N_KV_HEADS, s, HEAD_DIM), BF16),
            jax.ShapeDtypeStruct((b, N_KV_HEADS, s, HEAD_DIM), BF16),
            jax.ShapeDtypeStruct((b, s, Q_DIM), BF16),
        ],
        compiler_params=_compiler_params(2),
        name="attn_in_proj",
    )(x, mod, norm_g, w_in, gq, gk, cos_t, sin_t)


def _flash_kernel(q_ref, k_ref, v_ref, o_ref, *scratch, n_kv, tk, running_max):
    if running_max:
        m_sc, l_sc, acc_sc = scratch
        m_sc[...] = jnp.full(m_sc.shape, -jnp.inf, F32)
    else:
        l_sc, acc_sc = scratch
    l_sc[...] = jnp.zeros(l_sc.shape, F32)
    acc_sc[...] = jnp.zeros(acc_sc.shape, F32)
    q = q_ref[0, 0]
    n_lane_tiles = tk // V7X_LANES

    def body(j, carry):
        start = pl.multiple_of(j * tk, tk)
        ks = k_ref[0, 0, pl.ds(start, tk), :]
        vs = v_ref[0, 0, pl.ds(start, tk), :]
        s = lax.dot_general(q, ks, (((1,), (1,)), ((), ())), preferred_element_type=F32)
        if running_max:
            m_prev = m_sc[...]
            m_next = jnp.maximum(m_prev, jnp.max(s, axis=1, keepdims=True))
            alpha = jnp.exp2(m_prev - m_next)
            p = jnp.exp2(s - pltpu.repeat(m_next, n_lane_tiles, 1))
            l_sc[...] = alpha * l_sc[...] + jnp.sum(p, axis=1, keepdims=True)
            acc_sc[...] = acc_sc[...] * alpha + jnp.dot(p.astype(BF16), vs, preferred_element_type=F32)
            m_sc[...] = m_next
        else:
            p = jnp.exp2(s)
            part = p[:, :V7X_LANES]
            for c in range(1, n_lane_tiles):
                part = part + p[:, c * V7X_LANES:(c + 1) * V7X_LANES]
            l_sc[...] += part
            acc_sc[...] += jnp.dot(p.astype(BF16), vs, preferred_element_type=F32)
        return carry

    lax.fori_loop(0, n_kv, body, 0, unroll=KV_UNROLL)
    l = l_sc[...] if running_max else jnp.sum(l_sc[...], axis=1, keepdims=True)
    o_ref[0, 0] = (acc_sc[...] / l).astype(BF16)


def _flash_attention(q, k, v, *, running_max):
    b, nkv, sq, dh = q.shape
    s = k.shape[2]
    tq, tk = Q_TILE, KV_TILE
    kern = functools.partial(_flash_kernel, n_kv=s // tk, tk=tk, running_max=running_max)
    stat = pltpu.VMEM((tq, V7X_LANES), F32)
    return pl.pallas_call(
        kern,
        grid=(b, nkv, sq // tq),
        in_specs=[
            pl.BlockSpec((1, 1, tq, dh), lambda bi, g, i: (bi, g, i, 0)),
            pl.BlockSpec((1, 1, s, dh), lambda bi, g, i: (bi, g, 0, 0)),
            pl.BlockSpec((1, 1, s, dh), lambda bi, g, i: (bi, g, 0, 0)),
        ],
        out_specs=pl.BlockSpec((1, 1, tq, dh), lambda bi, g, i: (bi, g, i, 0)),
        out_shape=jax.ShapeDtypeStruct((b, nkv, sq, dh), BF16),
        scratch_shapes=([stat] if running_max else []) + [stat, pltpu.VMEM((tq, dh), F32)],
        compiler_params=_compiler_params(3),
        name="flash_attention_online" if running_max else "flash_attention",
    )(q, k, v)


def _attn_out_kernel(o_ref, sg_ref, x_ref, mod_ref, w_ref, y_ref):
    o = jnp.concatenate([o_ref[0, hh] for hh in range(N_HEADS)], axis=1)
    out = jnp.dot(o * sg_ref[0], w_ref[...], preferred_element_type=F32)
    y_ref[0] = x_ref[0] + mod_ref[2, 0] * out


def _attn_out_proj(o, sg, x, mod, w_out):
    b, s, _ = x.shape
    tm = ROW_TILE
    return pl.pallas_call(
        _attn_out_kernel,
        grid=(b, s // tm),
        in_specs=[
            pl.BlockSpec((1, N_HEADS, tm, HEAD_DIM), lambda bi, i: (bi, 0, i, 0)),
            pl.BlockSpec((1, tm, Q_DIM), lambda bi, i: (bi, i, 0)),
            pl.BlockSpec((1, tm, D_MODEL), lambda bi, i: (bi, i, 0)),
            pl.BlockSpec((3, 1, 1, D_MODEL), lambda bi, i: (0, bi, 0, 0)),
            pl.BlockSpec((Q_DIM, D_MODEL), lambda bi, i: (0, 0)),
        ],
        out_specs=pl.BlockSpec((1, tm, D_MODEL), lambda bi, i: (bi, i, 0)),
        out_shape=jax.ShapeDtypeStruct((b, s, D_MODEL), F32),
        compiler_params=_compiler_params(2),
        name="attn_out_proj",
    )(o, sg, x, mod, w_out)


def _pool_kernel(xp_ref, x_ref, xn_ref, mod_ref, ng_ref, win_ref, wg_ref, sc_ref, wout_ref, y_ref,
                 *, tm, seq):
    halo = POOL_HALO
    rows = tm + 2 * halo
    i = pl.program_id(1)
    x = x_ref[0]
    xx = jnp.concatenate([xp_ref[0], x, xn_ref[0]], axis=0)
    h = _modulated_rmsnorm(xx, ng_ref[...], mod_ref[1, 0], mod_ref[0, 0])
    y = jnp.dot(h.astype(BF16), win_ref[...], preferred_element_type=F32)
    t_all = lax.broadcasted_iota(jnp.int32, (rows, 1), 0) + (i * tm - halo)
    u = jnp.where((t_all >= 0) & (t_all < seq), y[:, :POOL_WIDTH], 0.0)
    gate = y[halo:halo + tm, POOL_WIDTH:]
    t = t_all[halo:halo + tm]

    def shifted(a, d):
        return pltpu.roll(a, (-d) % rows, 0)

    mixed = []
    for j, w in enumerate(POOL_WINDOWS):
        ug = u[:, j * POOL_GROUP_WIDTH:(j + 1) * POOL_GROUP_WIDTH]
        win = ug + shifted(ug, -1)
        span = 2
        while span < w:
            win = shifted(win, -(span // 2)) + shifted(win, span // 2)
            span *= 2
        cnt = jnp.minimum(t + w // 2, seq) - jnp.maximum(t - w // 2, 0)
        pooled = win[halo:halo + tm] * (1.0 / cnt.astype(F32))
        mix = (pooled - ug[halo:halo + tm]).astype(BF16)
        mixed.append(jnp.dot(mix, wg_ref[j], preferred_element_type=F32))
    z = jnp.concatenate(mixed, axis=1) * sc_ref[...] * _silu(gate)
    out = jnp.dot(z.astype(BF16), wout_ref[...], preferred_element_type=F32)
    y_ref[0] = x + mod_ref[2, 0] * out


def _pool_layer(x, mod, norm_g, w_in, w_group, scale, w_out):
    b, s, _ = x.shape
    tm = ROW_TILE
    halo = POOL_HALO
    per = tm // halo
    last = s // halo - 1
    const2 = lambda *_: (0, 0)
    kern = functools.partial(_pool_kernel, tm=tm, seq=s)
    return pl.pallas_call(
        kern,
        grid=(b, s // tm),
        in_specs=[
            pl.BlockSpec((1, halo, D_MODEL), lambda bi, i: (bi, jnp.maximum(i * per - 1, 0), 0)),
            pl.BlockSpec((1, tm, D_MODEL), lambda bi, i: (bi, i, 0)),
            pl.BlockSpec((1, halo, D_MODEL), lambda bi, i: (bi, jnp.minimum((i + 1) * per, last), 0)),
            pl.BlockSpec((3, 1, 1, D_MODEL), lambda bi, i: (0, bi, 0, 0)),
            pl.BlockSpec((1, D_MODEL), const2),
            pl.BlockSpec((D_MODEL, 2 * POOL_WIDTH), const2),
            pl.BlockSpec((N_POOL_GROUPS, POOL_GROUP_WIDTH, POOL_GROUP_WIDTH), lambda *_: (0, 0, 0)),
            pl.BlockSpec((1, POOL_WIDTH), const2),
            pl.BlockSpec((POOL_WIDTH, D_MODEL), const2),
        ],
        out_specs=pl.BlockSpec((1, tm, D_MODEL), lambda bi, i: (bi, i, 0)),
        out_shape=jax.ShapeDtypeStruct((b, s, D_MODEL), F32),
        compiler_params=_compiler_params(2),
        name="pool_layer",
    )(x, x, x, mod, norm_g, w_in, w_group, scale, w_out)


def _rope_tables(s):
    rows = s // GRID_W
    row = jnp.broadcast_to(jnp.arange(rows)[:, None], (rows, GRID_W)).reshape(s).astype(F32)
    col = jnp.broadcast_to(jnp.arange(GRID_W)[None, :], (rows, GRID_W)).reshape(s).astype(F32)
    inv = ROPE_THETA ** (-jnp.arange(0, AXIS_DIM, 2, dtype=F32) / AXIS_DIM)
    ang = jnp.concatenate([row[:, None] * inv, col[:, None] * inv], axis=-1)
    c, sn = jnp.cos(ang), jnp.sin(ang)
    return jnp.concatenate([c, c], axis=-1), jnp.concatenate([-sn, sn], axis=-1)


def _logit_bound(q_gain, k_gain):
    return (HEAD_DIM ** 0.5 * 1.01) * jnp.max(jnp.abs(q_gain)) * jnp.max(jnp.abs(k_gain))


def _deinterleave(a, n_heads):
    lead = a.shape[:-1]
    a = a.reshape(lead + (n_heads, HEAD_DIM // 2, 2))
    a = jnp.concatenate([a[..., 0], a[..., 1]], axis=-1)
    return a.reshape(lead + (n_heads * HEAD_DIM,))


def _trunk(x, mod, params):
    (norm_g, attn_w_in, gq, gk, small_logits, attn_w_out, cos_t, sin_t,
     pool_w_in, pool_w_group, pool_scale, pool_w_out) = params
    b, s, _ = x.shape
    q, k, v, sg = _attn_in_proj(x, mod[0], norm_g[0:1], attn_w_in, gq, gk, cos_t, sin_t)
    q = q.reshape(b, N_KV_HEADS, GROUP * s, HEAD_DIM)
    o = lax.cond(small_logits,
                 functools.partial(_flash_attention, running_max=False),
                 functools.partial(_flash_attention, running_max=True),
                 q, k, v)
    x = _attn_out_proj(o.reshape(b, N_HEADS, s, HEAD_DIM), sg, x, mod[0], attn_w_out)
    return _pool_layer(x, mod[1], norm_g[1:2], pool_w_in, pool_w_group, pool_scale, pool_w_out)


def kernel(x_prompt, x_sample, c_prompt, c_sample, norm_g, ada_w, ada_b, attn_w_in, attn_q_norm,
           attn_k_norm, attn_w_out, pool_w_in, pool_w_group, pool_scale, pool_w_out):
    assert DEPTH == 2 and attn_w_in.shape[0] == 1 and pool_w_in.shape[0] == 1
    bp, bs = x_prompt.shape[0], x_sample.shape[0]
    s = x_prompt.shape[1]
    assert x_sample.shape[1] == s and s % ROW_TILE == 0 and s % KV_TILE == 0

    c_all = jnp.concatenate([c_prompt, c_sample], axis=0)
    pad = (-c_all.shape[0]) % V7X_SUBLANES
    c_pad = jnp.pad(c_all, ((0, pad), (0, 0)))
    mod = _modulation(c_pad, ada_w, ada_b)
    mod = mod.reshape(DEPTH, 3, c_pad.shape[0], 1, D_MODEL)

    w_in = attn_w_in[0]
    w_in = jnp.concatenate(
        [_deinterleave(w_in[:, :Q_DIM], N_HEADS),
         _deinterleave(w_in[:, Q_DIM:Q_DIM + KV_DIM], N_KV_HEADS),
         w_in[:, Q_DIM + KV_DIM:]], axis=1).astype(BF16)
    gq = (_deinterleave(attn_q_norm[0], 1) * (HEAD_DIM ** -0.5 * LOG2_E)).reshape(1, HEAD_DIM)
    gk = _deinterleave(attn_k_norm[0], 1).reshape(1, HEAD_DIM)
    small_logits = _logit_bound(attn_q_norm[0], attn_k_norm[0]) <= MAX_UNSHIFTED_LOGIT
    cos_t, sin_t = _rope_tables(s)
    params = (norm_g, w_in, gq, gk, small_logits, attn_w_out[0].astype(BF16), cos_t, sin_t,
              pool_w_in[0].astype(BF16), pool_w_group[0].astype(BF16),
              pool_scale[0].reshape(1, POOL_WIDTH), pool_w_out[0].astype(BF16))

    y_prompt = _trunk(x_prompt, mod[:, :, :bp], params)
    y_sample = _trunk(x_sample, mod[:, :, bp:bp + bs], params)
    return (y_prompt, y_sample)
```

```python
import functools

import jax
import jax.numpy as jnp
from jax import lax
from jax.experimental import pallas as pl
from jax.experimental.pallas import tpu as pltpu

D_MODEL = 1024
DEPTH = 2
N_HEADS = 8
N_KV_HEADS = 2
HEAD_DIM = 128
GROUP = N_HEADS // N_KV_HEADS
Q_DIM = N_HEADS * HEAD_DIM
KV_DIM = N_KV_HEADS * HEAD_DIM
ATTN_IN = 2 * Q_DIM + 2 * KV_DIM
AXIS_DIM = HEAD_DIM // 2
ROPE_THETA = 10000.0
GRID_W = 64
POOL_WINDOWS = (2, 4, 8, 16)
N_POOL_GROUPS = len(POOL_WINDOWS)
POOL_WIDTH = D_MODEL
POOL_GROUP_WIDTH = POOL_WIDTH // N_POOL_GROUPS
RMS_EPS = 1e-6
LOG2_E = 1.4426950408889634
MAX_UNSHIFTED_LOGIT = 40.0

V7X_SUBLANES = 8
V7X_LANES = 128
V7X_VMEM_BYTES = 64 * 1024 * 1024
VMEM_LIMIT_BYTES = (V7X_VMEM_BYTES * 3) // 4

ROW_TILE = 512
Q_TILE = 1024
KV_TILE = ROW_TILE
POOL_HALO = max(POOL_WINDOWS) // 2

BF16 = jnp.bfloat16
F32 = jnp.float32
NT_DIMS = (((1,), (1,)), ((), ()))


def _compiler_params(n_grid_axes):
    return pltpu.CompilerParams(
        dimension_semantics=("arbitrary",) * n_grid_axes,
        vmem_limit_bytes=VMEM_LIMIT_BYTES,
    )


def _silu(x):
    return x * (1.0 / (1.0 + jnp.exp(-x)))


def _modulated_rmsnorm(x, norm_g, scl, shift):
    ms = jnp.mean(x * x, axis=-1, keepdims=True)
    return (x * lax.rsqrt(ms + RMS_EPS)) * norm_g * (1.0 + scl) + shift


def _mod_kernel(c_ref, w_ref, b_ref, o_ref):
    cs = _silu(c_ref[...])
    w = w_ref[0]
    cs_hi = cs.astype(BF16)
    cs_lo = (cs - cs_hi.astype(F32)).astype(BF16)
    w_hi = w.astype(BF16)
    w_lo = (w - w_hi.astype(F32)).astype(BF16)
    dot = functools.partial(jnp.dot, preferred_element_type=F32)
    o_ref[0, 0] = dot(cs_hi, w_hi) + (dot(cs_lo, w_hi) + dot(cs_hi, w_lo)) + b_ref[0, 0]


def _modulation(c_pad, ada_w, ada_b):
    bp = c_pad.shape[0]
    return pl.pallas_call(
        _mod_kernel,
        grid=(DEPTH, 3),
        in_specs=[
            pl.BlockSpec((bp, D_MODEL), lambda i, j: (0, 0)),
            pl.BlockSpec((1, D_MODEL, D_MODEL), lambda i, j: (i, 0, j)),
            pl.BlockSpec((1, 1, 1, D_MODEL), lambda i, j: (i, j, 0, 0)),
        ],
        out_specs=pl.BlockSpec((1, 1, bp, D_MODEL), lambda i, j: (i, j, 0, 0)),
        out_shape=jax.ShapeDtypeStruct((DEPTH, 3, bp, D_MODEL), F32),
        compiler_params=_compiler_params(2),
        name="adaln_modulation",
    )(c_pad, ada_w, ada_b.reshape(DEPTH, 3, 1, D_MODEL))


def _attn_in_kernel(x_ref, mod_ref, ng_ref, w_ref, wvt_ref, gq_ref, gk_ref, cos_ref, sin_ref,
                    q_ref, k_ref, vt_ref, sg_ref):
    h = _modulated_rmsnorm(x_ref[0], ng_ref[...], mod_ref[1, 0], mod_ref[0, 0]).astype(BF16)
    y = jnp.dot(h, w_ref[...], preferred_element_type=F32)
    vt = lax.dot_general(wvt_ref[...], h, NT_DIMS, preferred_element_type=F32)
    cos = cos_ref[...]
    sin = sin_ref[...]

    def norm_rope(t, g):
        ms = jnp.mean(t * t, axis=-1, keepdims=True)
        tn = t * lax.rsqrt(ms + RMS_EPS) * g
        return tn * cos + pltpu.roll(tn, HEAD_DIM // 2, 1) * sin

    for hh in range(N_HEADS):
        q_ref[0, hh] = norm_rope(y[:, hh * HEAD_DIM:(hh + 1) * HEAD_DIM], gq_ref[...]).astype(BF16)
    for j in range(N_KV_HEADS):
        lo = Q_DIM + j * HEAD_DIM
        k_ref[0, j] = norm_rope(y[:, lo:lo + HEAD_DIM], gk_ref[...]).astype(BF16)
        vt_ref[0, j, 0] = vt[j * HEAD_DIM:(j + 1) * HEAD_DIM].astype(BF16)
    sg_ref[0] = _silu(y[:, Q_DIM + KV_DIM:]).astype(BF16)


def _attn_in_proj(x, mod, norm_g, w_qkg, w_vt, gq, gk, cos_t, sin_t):
    b, s, _ = x.shape
    tm = ROW_TILE
    const = lambda *_: (0, 0)
    return pl.pallas_call(
        _attn_in_kernel,
        grid=(b, s // tm),
        in_specs=[
            pl.BlockSpec((1, tm, D_MODEL), lambda bi, i: (bi, i, 0)),
            pl.BlockSpec((3, 1, 1, D_MODEL), lambda bi, i: (0, bi, 0, 0)),
            pl.BlockSpec((1, D_MODEL), const),
            pl.BlockSpec((D_MODEL, ATTN_IN - KV_DIM), const),
            pl.BlockSpec((KV_DIM, D_MODEL), const),
            pl.BlockSpec((1, HEAD_DIM), const),
            pl.BlockSpec((1, HEAD_DIM), const),
            pl.BlockSpec((tm, HEAD_DIM), lambda bi, i: (i, 0)),
            pl.BlockSpec((tm, HEAD_DIM), lambda bi, i: (i, 0)),
        ],
        out_specs=[
            pl.BlockSpec((1, N_HEADS, tm, HEAD_DIM), lambda bi, i: (bi, 0, i, 0)),
            pl.BlockSpec((1, N_KV_HEADS, tm, HEAD_DIM), lambda bi, i: (bi, 0, i, 0)),
            pl.BlockSpec((1, N_KV_HEADS, 1, HEAD_DIM, tm), lambda bi, i: (bi, 0, i, 0, 0)),
            pl.BlockSpec((1, tm, Q_DIM), lambda bi, i: (bi, i, 0)),
        ],
        out_shape=[
            jax.ShapeDtypeStruct((b, N_HEADS, s, HEAD_DIM), BF16),
            jax.ShapeDtypeStruct((b, N_KV_HEADS, s, HEAD_DIM), BF16),
            jax.ShapeDtypeStruct((b, N_KV_HEADS, s // tm, HEAD_DIM, tm), BF16),
            jax.ShapeDtypeStruct((b, s, Q_DIM), BF16),
        ],
        compiler_params=_compiler_params(2),
        name="attn_in_proj",
    )(x, mod, norm_g, w_qkg, w_vt, gq, gk, cos_t, sin_t)


def _flash_kernel(q_ref, k_ref, vt_ref, o_ref, *scratch, n_kv, tk, running_max):
    q = q_ref[0, 0]
    tq = q.shape[0]

    def scores_t(j):
        return lax.dot_general(k_ref[0, 0, pl.ds(j * tk, tk), :], q, NT_DIMS,
                               preferred_element_type=F32)

    if running_max:
        m_sc, l_sc, acc_sc = scratch
        m_sc[...] = jnp.full(m_sc.shape, -jnp.inf, F32)
        l_sc[...] = jnp.zeros(l_sc.shape, F32)
        acc_sc[...] = jnp.zeros(acc_sc.shape, F32)

        def body(j, carry):
            s_t = scores_t(j)
            m_prev = m_sc[...]
            m_next = jnp.maximum(m_prev, jnp.max(s_t, axis=0, keepdims=True))
            alpha = jnp.exp2(m_prev - m_next)
            p_t = jnp.exp2(s_t - m_next)
            l_sc[...] = alpha * l_sc[...] + jnp.sum(p_t, axis=0, keepdims=True)
            acc_sc[...] = acc_sc[...] * alpha + jnp.dot(
                vt_ref[0, 0, j], p_t.astype(BF16), preferred_element_type=F32)
            m_sc[...] = m_next
            return carry

        lax.fori_loop(0, n_kv, body, 0)
        o_t = acc_sc[...] / l_sc[...]
    else:
        acc = jnp.zeros((HEAD_DIM, tq), F32)
        l_part = jnp.zeros((V7X_SUBLANES, tq), F32)
        s_next = scores_t(0)
        for j in range(n_kv):
            s_t = s_next
            if j + 1 < n_kv:
                s_next = scores_t(j + 1)
            p_t = jnp.exp2(s_t)
            l_part = l_part + jnp.sum(p_t.reshape(tk // V7X_SUBLANES, V7X_SUBLANES, tq), axis=0)
            acc = acc + jnp.dot(vt_ref[0, 0, j], p_t.astype(BF16), preferred_element_type=F32)
        o_t = acc / jnp.sum(l_part, axis=0, keepdims=True)
    o_ref[0, 0] = o_t.T.astype(BF16)


def _flash_attention(q, k, vt, *, running_max):
    b, nkv, sq, dh = q.shape
    s = k.shape[2]
    n_kv, tk = vt.shape[2], vt.shape[4]
    tq = Q_TILE
    kern = functools.partial(_flash_kernel, n_kv=n_kv, tk=tk, running_max=running_max)
    stat = pltpu.VMEM((1, tq), F32)
    return pl.pallas_call(
        kern,
        grid=(b, nkv, sq // tq),
        in_specs=[
            pl.BlockSpec((1, 1, tq, dh), lambda bi, g, i: (bi, g, i, 0)),
            pl.BlockSpec((1, 1, s, dh), lambda bi, g, i: (bi, g, 0, 0)),
            pl.BlockSpec((1, 1, n_kv, dh, tk), lambda bi, g, i: (bi, g, 0, 0, 0)),
        ],
        out_specs=pl.BlockSpec((1, 1, tq, dh), lambda bi, g, i: (bi, g, i, 0)),
        out_shape=jax.ShapeDtypeStruct((b, nkv, sq, dh), BF16),
        scratch_shapes=[stat, stat, pltpu.VMEM((dh, tq), F32)] if running_max else [],
        compiler_params=_compiler_params(3),
        name="flash_attention_online" if running_max else "flash_attention",
    )(q, k, vt)


def _attn_out_kernel(o_ref, sg_ref, x_ref, mod_ref, w_ref, y_ref):
    o = jnp.concatenate([o_ref[0, hh] for hh in range(N_HEADS)], axis=1)
    out = jnp.dot(o * sg_ref[0], w_ref[...], preferred_element_type=F32)
    y_ref[0] = x_ref[0] + mod_ref[2, 0] * out


def _attn_out_proj(o, sg, x, mod, w_out):
    b, s, _ = x.shape
    tm = ROW_TILE
    return pl.pallas_call(
        _attn_out_kernel,
        grid=(b, s // tm),
        in_specs=[
            pl.BlockSpec((1, N_HEADS, tm, HEAD_DIM), lambda bi, i: (bi, 0, i, 0)),
            pl.BlockSpec((1, tm, Q_DIM), lambda bi, i: (bi, i, 0)),
            pl.BlockSpec((1, tm, D_MODEL), lambda bi, i: (bi, i, 0)),
            pl.BlockSpec((3, 1, 1, D_MODEL), lambda bi, i: (0, bi, 0, 0)),
            pl.BlockSpec((Q_DIM, D_MODEL), lambda bi, i: (0, 0)),
        ],
        out_specs=pl.BlockSpec((1, tm, D_MODEL), lambda bi, i: (bi, i, 0)),
        out_shape=jax.ShapeDtypeStruct((b, s, D_MODEL), F32),
        compiler_params=_compiler_params(2),
        name="attn_out_proj",
    )(o, sg, x, mod, w_out)


def _pool_kernel(xp_ref, x_ref, xn_ref, mod_ref, ng_ref, win_ref, wg_ref, sc_ref, wout_ref, y_ref,
                 *, tm, seq):
    halo = POOL_HALO
    rows = tm + 2 * halo
    i = pl.program_id(1)
    x = x_ref[0]
    xx = jnp.concatenate([xp_ref[0], x, xn_ref[0]], axis=0)
    h = _modulated_rmsnorm(xx, ng_ref[...], mod_ref[1, 0], mod_ref[0, 0])
    y = jnp.dot(h.astype(BF16), win_ref[...], preferred_element_type=F32)
    t_all = lax.broadcasted_iota(jnp.int32, (rows, 1), 0) + (i * tm - halo)
    u = jnp.where((t_all >= 0) & (t_all < seq), y[:, :POOL_WIDTH], 0.0)
    gate = y[halo:halo + tm, POOL_WIDTH:]
    t = t_all[halo:halo + tm]

    def shifted(a, d):
        return pltpu.roll(a, (-d) % rows, 0)

    mixed = []
    for j, w in enumerate(POOL_WINDOWS):
        ug = u[:, j * POOL_GROUP_WIDTH:(j + 1) * POOL_GROUP_WIDTH]
        win = ug + shifted(ug, -1)
        span = 2
        while span < w:
            win = shifted(win, -(span // 2)) + shifted(win, span // 2)
            span *= 2
        cnt = jnp.minimum(t + w // 2, seq) - jnp.maximum(t - w // 2, 0)
        pooled = win[halo:halo + tm] * (1.0 / cnt.astype(F32))
        mix = (pooled - ug[halo:halo + tm]).astype(BF16)
        mixed.append(jnp.dot(mix, wg_ref[j], preferred_element_type=F32))
    z = jnp.concatenate(mixed, axis=1) * sc_ref[...] * _silu(gate)
    out = jnp.dot(z.astype(BF16), wout_ref[...], preferred_element_type=F32)
    y_ref[0] = x + mod_ref[2, 0] * out


def _pool_layer(x, mod, norm_g, w_in, w_group, scale, w_out):
    b, s, _ = x.shape
    tm = ROW_TILE
    halo = POOL_HALO
    per = tm // halo
    last = s // halo - 1
    const2 = lambda *_: (0, 0)
    kern = functools.partial(_pool_kernel, tm=tm, seq=s)
    return pl.pallas_call(
        kern,
        grid=(b, s // tm),
        in_specs=[
            pl.BlockSpec((1, halo, D_MODEL), lambda bi, i: (bi, jnp.maximum(i * per - 1, 0), 0)),
            pl.BlockSpec((1, tm, D_MODEL), lambda bi, i: (bi, i, 0)),
            pl.BlockSpec((1, halo, D_MODEL), lambda bi, i: (bi, jnp.minimum((i + 1) * per, last), 0)),
            pl.BlockSpec((3, 1, 1, D_MODEL), lambda bi, i: (0, bi, 0, 0)),
            pl.BlockSpec((1, D_MODEL), const2),
            pl.BlockSpec((D_MODEL, 2 * POOL_WIDTH), const2),
            pl.BlockSpec((N_POOL_GROUPS, POOL_GROUP_WIDTH, POOL_GROUP_WIDTH), lambda *_: (0, 0, 0)),
            pl.BlockSpec((1, POOL_WIDTH), const2),
            pl.BlockSpec((POOL_WIDTH, D_MODEL), const2),
        ],
        out_specs=pl.BlockSpec((1, tm, D_MODEL), lambda bi, i: (bi, i, 0)),
        out_shape=jax.ShapeDtypeStruct((b, s, D_MODEL), F32),
        compiler_params=_compiler_params(2),
        name="pool_layer",
    )(x, x, x, mod, norm_g, w_in, w_group, scale, w_out)


def _rope_tables(s):
    rows = s // GRID_W
    row = jnp.broadcast_to(jnp.arange(rows)[:, None], (rows, GRID_W)).reshape(s).astype(F32)
    col = jnp.broadcast_to(jnp.arange(GRID_W)[None, :], (rows, GRID_W)).reshape(s).astype(F32)
    inv = ROPE_THETA ** (-jnp.arange(0, AXIS_DIM, 2, dtype=F32) / AXIS_DIM)
    ang = jnp.concatenate([row[:, None] * inv, col[:, None] * inv], axis=-1)
    c, sn = jnp.cos(ang), jnp.sin(ang)
    return jnp.concatenate([c, c], axis=-1), jnp.concatenate([-sn, sn], axis=-1)


def _logit_bound(q_gain, k_gain):
    return (HEAD_DIM ** 0.5 * 1.01) * jnp.max(jnp.abs(q_gain)) * jnp.max(jnp.abs(k_gain))


def _deinterleave(a, n_heads):
    lead = a.shape[:-1]
    a = a.reshape(lead + (n_heads, HEAD_DIM // 2, 2))
    a = jnp.concatenate([a[..., 0], a[..., 1]], axis=-1)
    return a.reshape(lead + (n_heads * HEAD_DIM,))


def _trunk(x, mod, params):
    (norm_g, w_qkg, w_vt, gq, gk, small_logits, attn_w_out, cos_t, sin_t,
     pool_w_in, pool_w_group, pool_scale, pool_w_out) = params
    b, s, _ = x.shape
    q, k, vt, sg = _attn_in_proj(x, mod[0], norm_g[0:1], w_qkg, w_vt, gq, gk, cos_t, sin_t)
    q = q.reshape(b, N_KV_HEADS, GROUP * s, HEAD_DIM)
    o = lax.cond(small_logits,
                 functools.partial(_flash_attention, running_max=False),
                 functools.partial(_flash_attention, running_max=True),
                 q, k, vt)
    x = _attn_out_proj(o.reshape(b, N_HEADS, s, HEAD_DIM), sg, x, mod[0], attn_w_out)
    return _pool_layer(x, mod[1], norm_g[1:2], pool_w_in, pool_w_group, pool_scale, pool_w_out)


def kernel(x_prompt, x_sample, c_prompt, c_sample, norm_g, ada_w, ada_b, attn_w_in, attn_q_norm,
           attn_k_norm, attn_w_out, pool_w_in, pool_w_group, pool_scale, pool_w_out):
    assert DEPTH == 2 and attn_w_in.shape[0] == 1 and pool_w_in.shape[0] == 1
    bp, bs = x_prompt.shape[0], x_sample.shape[0]
    s = x_prompt.shape[1]
    assert x_sample.shape[1] == s and s % ROW_TILE == 0 and s % KV_TILE == 0

    c_all = jnp.concatenate([c_prompt, c_sample], axis=0)
    pad = (-c_all.shape[0]) % V7X_SUBLANES
    c_pad = jnp.pad(c_all, ((0, pad), (0, 0)))
    mod = _modulation(c_pad, ada_w, ada_b)
    mod = mod.reshape(DEPTH, 3, c_pad.shape[0], 1, D_MODEL)

    w_in = attn_w_in[0]
    w_qkg = jnp.concatenate(
        [_deinterleave(w_in[:, :Q_DIM], N_HEADS),
         _deinterleave(w_in[:, Q_DIM:Q_DIM + KV_DIM], N_KV_HEADS),
         w_in[:, Q_DIM + 2 * KV_DIM:]], axis=1).astype(BF16)
    w_vt = w_in[:, Q_DIM + KV_DIM:Q_DIM + 2 * KV_DIM].T.astype(BF16)
    gq = (_deinterleave(attn_q_norm[0], 1) * (HEAD_DIM ** -0.5 * LOG2_E)).reshape(1, HEAD_DIM)
    gk = _deinterleave(attn_k_norm[0], 1).reshape(1, HEAD_DIM)
    small_logits = _logit_bound(attn_q_norm[0], attn_k_norm[0]) <= MAX_UNSHIFTED_LOGIT
    cos_t, sin_t = _rope_tables(s)
    params = (norm_g, w_qkg, w_vt, gq, gk, small_logits, attn_w_out[0].astype(BF16), cos_t, sin_t,
              pool_w_in[0].astype(BF16), pool_w_group[0].astype(BF16),
              pool_scale[0].reshape(1, POOL_WIDTH), pool_w_out[0].astype(BF16))

    y_prompt = _trunk(x_prompt, mod[:, :, :bp], params)
    y_sample = _trunk(x_sample, mod[:, :, bp:bp + bs], params)
    return (y_prompt, y_sample)
```

```python
import functools

import jax
import jax.numpy as jnp
from jax import lax
from jax.experimental import pallas as pl
from jax.experimental.pallas import tpu as pltpu

D_MODEL = 1024
DEPTH = 2
N_HEADS = 8
N_KV_HEADS = 2
HEAD_DIM = 128
GROUP = N_HEADS // N_KV_HEADS
Q_DIM = N_HEADS * HEAD_DIM
KV_DIM = N_KV_HEADS * HEAD_DIM
ATTN_IN = 2 * Q_DIM + 2 * KV_DIM
AXIS_DIM = HEAD_DIM // 2
ROPE_THETA = 10000.0
GRID_W = 64
POOL_WINDOWS = (2, 4, 8, 16)
N_POOL_GROUPS = len(POOL_WINDOWS)
POOL_WIDTH = D_MODEL
POOL_GROUP_WIDTH = POOL_WIDTH // N_POOL_GROUPS
RMS_EPS = 1e-6
LOG2_E = 1.4426950408889634
MAX_UNSHIFTED_LOGIT = 40.0

V7X_SUBLANES = 8
V7X_LANES = 128
MXU_WIDTH = 256
V7X_VMEM_BYTES = 64 * 1024 * 1024
VMEM_LIMIT_BYTES = (V7X_VMEM_BYTES * 3) // 4

ROW_TILE = 512
Q_TILE = 1024
KV_TILE = ROW_TILE
POOL_HALO = max(POOL_WINDOWS) // 2

BF16 = jnp.bfloat16
F32 = jnp.float32
NT_DIMS = (((1,), (1,)), ((), ()))


def _compiler_params(n_grid_axes):
    return pltpu.CompilerParams(
        dimension_semantics=("arbitrary",) * n_grid_axes,
        vmem_limit_bytes=VMEM_LIMIT_BYTES,
    )


def _silu(x):
    return x * (1.0 / (1.0 + jnp.exp(-x)))


def _modulated_rmsnorm(x, norm_g, scl, shift):
    ms = jnp.mean(x * x, axis=-1, keepdims=True)
    return (x * lax.rsqrt(ms + RMS_EPS)) * norm_g * (1.0 + scl) + shift


def _mod_kernel(c_ref, w_ref, b_ref, o_ref):
    cs = _silu(c_ref[...])
    w = w_ref[0]
    cs_hi = cs.astype(BF16)
    cs_lo = (cs - cs_hi.astype(F32)).astype(BF16)
    w_hi = w.astype(BF16)
    w_lo = (w - w_hi.astype(F32)).astype(BF16)
    dot = functools.partial(jnp.dot, preferred_element_type=F32)
    o_ref[0, 0] = dot(cs_hi, w_hi) + (dot(cs_lo, w_hi) + dot(cs_hi, w_lo)) + b_ref[0, 0]


def _modulation(c_pad, ada_w, ada_b):
    bp = c_pad.shape[0]
    return pl.pallas_call(
        _mod_kernel,
        grid=(DEPTH, 3),
        in_specs=[
            pl.BlockSpec((bp, D_MODEL), lambda i, j: (0, 0)),
            pl.BlockSpec((1, D_MODEL, D_MODEL), lambda i, j: (i, 0, j)),
            pl.BlockSpec((1, 1, 1, D_MODEL), lambda i, j: (i, j, 0, 0)),
        ],
        out_specs=pl.BlockSpec((1, 1, bp, D_MODEL), lambda i, j: (i, j, 0, 0)),
        out_shape=jax.ShapeDtypeStruct((DEPTH, 3, bp, D_MODEL), F32),
        compiler_params=_compiler_params(2),
        name="adaln_modulation",
    )(c_pad, ada_w, ada_b.reshape(DEPTH, 3, 1, D_MODEL))


def _attn_in_kernel(x_ref, mod_ref, ng_ref, wt_ref, w_ref, qa_ref, qb_ref, ka_ref, kb_ref,
                    qt_ref, k_ref, vt_ref, sg_ref):
    h = _modulated_rmsnorm(x_ref[0], ng_ref[...], mod_ref[1, 0], mod_ref[0, 0]).astype(BF16)
    yt = lax.dot_general(wt_ref[...], h, NT_DIMS, preferred_element_type=F32)
    y = jnp.dot(h, w_ref[...], preferred_element_type=F32)
    half = HEAD_DIM // 2
    qa = qa_ref[...]
    qb = qb_ref[...]
    for hh in range(N_HEADS):
        t = yt[hh * HEAD_DIM:(hh + 1) * HEAD_DIM]
        rs = lax.rsqrt(jnp.mean(t * t, axis=0, keepdims=True) + RMS_EPS)
        partner = jnp.concatenate([t[half:], t[:half]], axis=0)
        qt_ref[0, hh] = ((t * qa + partner * qb) * rs).astype(BF16)
    for j in range(N_KV_HEADS):
        lo = Q_DIM + j * HEAD_DIM
        vt_ref[0, j, 0] = yt[lo:lo + HEAD_DIM].astype(BF16)
        t = y[:, j * HEAD_DIM:(j + 1) * HEAD_DIM]
        rs = lax.rsqrt(jnp.mean(t * t, axis=1, keepdims=True) + RMS_EPS)
        k_ref[0, j] = ((t * ka_ref[...] + pltpu.roll(t, half, 1) * kb_ref[...]) * rs).astype(BF16)
    sg_ref[0] = _silu(y[:, KV_DIM:]).astype(BF16)


def _attn_in_proj(x, mod, norm_g, w_t, w_kg, rope_q, rope_k):
    b, s, _ = x.shape
    tm = ROW_TILE
    const = lambda *_: (0, 0)
    return pl.pallas_call(
        _attn_in_kernel,
        grid=(b, s // tm),
        in_specs=[
            pl.BlockSpec((1, tm, D_MODEL), lambda bi, i: (bi, i, 0)),
            pl.BlockSpec((3, 1, 1, D_MODEL), lambda bi, i: (0, bi, 0, 0)),
            pl.BlockSpec((1, D_MODEL), const),
            pl.BlockSpec((Q_DIM + KV_DIM, D_MODEL), const),
            pl.BlockSpec((D_MODEL, KV_DIM + Q_DIM), const),
            pl.BlockSpec((HEAD_DIM, tm), lambda bi, i: (0, i)),
            pl.BlockSpec((HEAD_DIM, tm), lambda bi, i: (0, i)),
            pl.BlockSpec((tm, HEAD_DIM), lambda bi, i: (i, 0)),
            pl.BlockSpec((tm, HEAD_DIM), lambda bi, i: (i, 0)),
        ],
        out_specs=[
            pl.BlockSpec((1, N_HEADS, HEAD_DIM, tm), lambda bi, i: (bi, 0, 0, i)),
            pl.BlockSpec((1, N_KV_HEADS, tm, HEAD_DIM), lambda bi, i: (bi, 0, i, 0)),
            pl.BlockSpec((1, N_KV_HEADS, 1, HEAD_DIM, tm), lambda bi, i: (bi, 0, i, 0, 0)),
            pl.BlockSpec((1, tm, Q_DIM), lambda bi, i: (bi, i, 0)),
        ],
        out_shape=[
            jax.ShapeDtypeStruct((b, N_HEADS, HEAD_DIM, s), BF16),
            jax.ShapeDtypeStruct((b, N_KV_HEADS, s, HEAD_DIM), BF16),
            jax.ShapeDtypeStruct((b, N_KV_HEADS, s // tm, HEAD_DIM, tm), BF16),
            jax.ShapeDtypeStruct((b, s, Q_DIM), BF16),
        ],
        compiler_params=_compiler_params(2),
        name="attn_in_proj",
    )(x, mod, norm_g, w_t, w_kg, *rope_q, *rope_k)


def _flash_kernel(qt_ref, k_ref, vt_ref, o_ref, *scratch, n_kv, tk, running_max):
    q_t = qt_ref[0, 0]
    tq = q_t.shape[1]

    def scores_t(j):
        return jnp.dot(k_ref[0, 0, pl.ds(j * tk, tk), :], q_t, preferred_element_type=F32)

    if running_max:
        m_sc, l_sc, acc_sc = scratch
        m_sc[...] = jnp.full(m_sc.shape, -jnp.inf, F32)
        l_sc[...] = jnp.zeros(l_sc.shape, F32)
        acc_sc[...] = jnp.zeros(acc_sc.shape, F32)

        def body(j, carry):
            s_t = scores_t(j)
            m_prev = m_sc[...]
            m_next = jnp.maximum(m_prev, jnp.max(s_t, axis=0, keepdims=True))
            alpha = jnp.exp2(m_prev - m_next)
            p_t = jnp.exp2(s_t - m_next)
            l_sc[...] = alpha * l_sc[...] + jnp.sum(p_t, axis=0, keepdims=True)
            acc_sc[...] = acc_sc[...] * alpha + jnp.dot(
                vt_ref[0, 0, j], p_t.astype(BF16), preferred_element_type=F32)
            m_sc[...] = m_next
            return carry

        lax.fori_loop(0, n_kv, body, 0)
        o_t = acc_sc[...] / l_sc[...]
    else:
        acc = jnp.zeros((HEAD_DIM, tq), F32)
        l_part = jnp.zeros((V7X_SUBLANES, tq), F32)
        s_next = scores_t(0)
        for j in range(n_kv):
            s_t = s_next
            if j + 1 < n_kv:
                s_next = scores_t(j + 1)
            p_t = jnp.exp2(s_t)
            l_part = l_part + jnp.sum(p_t.reshape(tk // V7X_SUBLANES, V7X_SUBLANES, tq), axis=0)
            acc = acc + jnp.dot(vt_ref[0, 0, j], p_t.astype(BF16), preferred_element_type=F32)
        o_t = acc / jnp.sum(l_part, axis=0, keepdims=True)
    o_ref[0, 0] = o_t.T.astype(BF16)


def _flash_attention(qt, k, vt, *, running_max):
    b, nh, dh, s = qt.shape
    nkv, n_kv, tk = vt.shape[1], vt.shape[2], vt.shape[4]
    group = nh // nkv
    tq = Q_TILE
    n_q = s // tq
    kern = functools.partial(_flash_kernel, n_kv=n_kv, tk=tk, running_max=running_max)
    stat = pltpu.VMEM((1, tq), F32)
    return pl.pallas_call(
        kern,
        grid=(b, nkv, group * n_q),
        in_specs=[
            pl.BlockSpec((1, 1, dh, tq), lambda bi, g, i: (bi, g * group + i // n_q, 0, i % n_q)),
            pl.BlockSpec((1, 1, s, dh), lambda bi, g, i: (bi, g, 0, 0)),
            pl.BlockSpec((1, 1, n_kv, dh, tk), lambda bi, g, i: (bi, g, 0, 0, 0)),
        ],
        out_specs=pl.BlockSpec((1, 1, tq, dh), lambda bi, g, i: (bi, g * group + i // n_q, i % n_q, 0)),
        out_shape=jax.ShapeDtypeStruct((b, nh, s, dh), BF16),
        scratch_shapes=[stat, stat, pltpu.VMEM((dh, tq), F32)] if running_max else [],
        compiler_params=_compiler_params(3),
        name="flash_attention_online" if running_max else "flash_attention",
    )(qt, k, vt)


def _attn_out_kernel(o_ref, sg_ref, x_ref, mod_ref, w_ref, y_ref):
    o = jnp.concatenate([o_ref[0, hh] for hh in range(N_HEADS)], axis=1)
    out = jnp.dot(o * sg_ref[0], w_ref[...], preferred_element_type=F32)
    y_ref[0] = x_ref[0] + mod_ref[2, 0] * out


def _attn_out_proj(o, sg, x, mod, w_out):
    b, s, _ = x.shape
    tm = ROW_TILE
    return pl.pallas_call(
        _attn_out_kernel,
        grid=(b, s // tm),
        in_specs=[
            pl.BlockSpec((1, N_HEADS, tm, HEAD_DIM), lambda bi, i: (bi, 0, i, 0)),
            pl.BlockSpec((1, tm, Q_DIM), lambda bi, i: (bi, i, 0)),
            pl.BlockSpec((1, tm, D_MODEL), lambda bi, i: (bi, i, 0)),
            pl.BlockSpec((3, 1, 1, D_MODEL), lambda bi, i: (0, bi, 0, 0)),
            pl.BlockSpec((Q_DIM, D_MODEL), lambda bi, i: (0, 0)),
        ],
        out_specs=pl.BlockSpec((1, tm, D_MODEL), lambda bi, i: (bi, i, 0)),
        out_shape=jax.ShapeDtypeStruct((b, s, D_MODEL), F32),
        compiler_params=_compiler_params(2),
        name="attn_out_proj",
    )(o, sg, x, mod, w_out)


def _pool_kernel(xp_ref, x_ref, xn_ref, mod_ref, ng_ref, win_ref, wg_ref, sc_ref, wout_ref, y_ref,
                 *, tm, seq):
    halo = POOL_HALO
    rows = tm + 2 * halo
    i = pl.program_id(1)
    x = x_ref[0]
    xx = jnp.concatenate([xp_ref[0], x, xn_ref[0]], axis=0)
    h = _modulated_rmsnorm(xx, ng_ref[...], mod_ref[1, 0], mod_ref[0, 0])
    y = jnp.dot(h.astype(BF16), win_ref[...], preferred_element_type=F32)
    t_all = lax.broadcasted_iota(jnp.int32, (rows, 1), 0) + (i * tm - halo)
    u = jnp.where((t_all >= 0) & (t_all < seq), y[:, :POOL_WIDTH], 0.0)
    gate = y[halo:halo + tm, POOL_WIDTH:]
    t = t_all[halo:halo + tm]

    def shifted(a, d):
        return pltpu.roll(a, (-d) % rows, 0)

    mixed = []
    for j, w in enumerate(POOL_WINDOWS):
        ug = u[:, j * POOL_GROUP_WIDTH:(j + 1) * POOL_GROUP_WIDTH]
        win = ug + shifted(ug, -1)
        span = 2
        while span < w:
            win = shifted(win, -(span // 2)) + shifted(win, span // 2)
            span *= 2
        cnt = jnp.minimum(t + w // 2, seq) - jnp.maximum(t - w // 2, 0)
        pooled = win[halo:halo + tm] * (1.0 / cnt.astype(F32))
        mix = (pooled - ug[halo:halo + tm]).astype(BF16)
        mixed.append(jnp.dot(mix, wg_ref[j], preferred_element_type=F32))
    z = jnp.concatenate(mixed, axis=1) * sc_ref[...] * _silu(gate)
    out = jnp.dot(z.astype(BF16), wout_ref[...], preferred_element_type=F32)
    y_ref[0] = x + mod_ref[2, 0] * out


def _pool_layer(x, mod, norm_g, w_in, w_group, scale, w_out):
    b, s, _ = x.shape
    tm = ROW_TILE
    halo = POOL_HALO
    per = tm // halo
    last = s // halo - 1
    const2 = lambda *_: (0, 0)
    kern = functools.partial(_pool_kernel, tm=tm, seq=s)
    return pl.pallas_call(
        kern,
        grid=(b, s // tm),
        in_specs=[
            pl.BlockSpec((1, halo, D_MODEL), lambda bi, i: (bi, jnp.maximum(i * per - 1, 0), 0)),
            pl.BlockSpec((1, tm, D_MODEL), lambda bi, i: (bi, i, 0)),
            pl.BlockSpec((1, halo, D_MODEL), lambda bi, i: (bi, jnp.minimum((i + 1) * per, last), 0)),
            pl.BlockSpec((3, 1, 1, D_MODEL), lambda bi, i: (0, bi, 0, 0)),
            pl.BlockSpec((1, D_MODEL), const2),
            pl.BlockSpec((D_MODEL, 2 * POOL_WIDTH), const2),
            pl.BlockSpec((N_POOL_GROUPS, POOL_GROUP_WIDTH, POOL_GROUP_WIDTH), lambda *_: (0, 0, 0)),
            pl.BlockSpec((1, POOL_WIDTH), const2),
            pl.BlockSpec((POOL_WIDTH, D_MODEL), const2),
        ],
        out_specs=pl.BlockSpec((1, tm, D_MODEL), lambda bi, i: (bi, i, 0)),
        out_shape=jax.ShapeDtypeStruct((b, s, D_MODEL), F32),
        compiler_params=_compiler_params(2),
        name="pool_layer",
    )(x, x, x, mod, norm_g, w_in, w_group, scale, w_out)


def _rope_tables(s):
    rows = s // GRID_W
    row = jnp.broadcast_to(jnp.arange(rows)[:, None], (rows, GRID_W)).reshape(s).astype(F32)
    col = jnp.broadcast_to(jnp.arange(GRID_W)[None, :], (rows, GRID_W)).reshape(s).astype(F32)
    inv = ROPE_THETA ** (-jnp.arange(0, AXIS_DIM, 2, dtype=F32) / AXIS_DIM)
    ang = jnp.concatenate([row[:, None] * inv, col[:, None] * inv], axis=-1)
    c, sn = jnp.cos(ang), jnp.sin(ang)
    return jnp.concatenate([c, c], axis=-1), jnp.concatenate([-sn, sn], axis=-1)


def _gained_rope_tables(cos_t, sin_t, gain):
    half = HEAD_DIM // 2
    partner_gain = jnp.concatenate([gain[half:], gain[:half]])
    return cos_t * gain, sin_t * partner_gain


def _logit_bound(q_gain, k_gain):
    return (HEAD_DIM ** 0.5 * 1.01) * jnp.max(jnp.abs(q_gain)) * jnp.max(jnp.abs(k_gain))


def _deinterleave(a, n_heads):
    lead = a.shape[:-1]
    a = a.reshape(lead + (n_heads, HEAD_DIM // 2, 2))
    a = jnp.concatenate([a[..., 0], a[..., 1]], axis=-1)
    return a.reshape(lead + (n_heads * HEAD_DIM,))


def _trunk(x, mod, params):
    (norm_g, w_t, w_kg, rope_q, rope_k, small_logits, attn_w_out,
     pool_w_in, pool_w_group, pool_scale, pool_w_out) = params
    qt, k, vt, sg = _attn_in_proj(x, mod[0], norm_g[0:1], w_t, w_kg, rope_q, rope_k)
    o = lax.cond(small_logits,
                 functools.partial(_flash_attention, running_max=False),
                 functools.partial(_flash_attention, running_max=True),
                 qt, k, vt)
    x = _attn_out_proj(o, sg, x, mod[0], attn_w_out)
    return _pool_layer(x, mod[1], norm_g[1:2], pool_w_in, pool_w_group, pool_scale, pool_w_out)


def kernel(x_prompt, x_sample, c_prompt, c_sample, norm_g, ada_w, ada_b, attn_w_in, attn_q_norm,
           attn_k_norm, attn_w_out, pool_w_in, pool_w_group, pool_scale, pool_w_out):
    assert DEPTH == 2 and attn_w_in.shape[0] == 1 and pool_w_in.shape[0] == 1
    bp, bs = x_prompt.shape[0], x_sample.shape[0]
    s = x_prompt.shape[1]
    assert x_sample.shape[1] == s and s % ROW_TILE == 0 and s % KV_TILE == 0

    c_all = jnp.concatenate([c_prompt, c_sample], axis=0)
    pad = (-c_all.shape[0]) % V7X_SUBLANES
    c_pad = jnp.pad(c_all, ((0, pad), (0, 0)))
    mod = _modulation(c_pad, ada_w, ada_b)
    mod = mod.reshape(DEPTH, 3, c_pad.shape[0], 1, D_MODEL)

    w_in = attn_w_in[0]
    w_q = _deinterleave(w_in[:, :Q_DIM], N_HEADS)
    w_k = _deinterleave(w_in[:, Q_DIM:Q_DIM + KV_DIM], N_KV_HEADS)
    w_v = w_in[:, Q_DIM + KV_DIM:Q_DIM + 2 * KV_DIM]
    w_t = jnp.concatenate([w_q, w_v], axis=1).T.astype(BF16)
    w_kg = jnp.concatenate([w_k, w_in[:, Q_DIM + 2 * KV_DIM:]], axis=1).astype(BF16)
    cos_t, sin_t = _rope_tables(s)
    gq = _deinterleave(attn_q_norm[0], 1) * (HEAD_DIM ** -0.5 * LOG2_E)
    rope_q = tuple(t.T for t in _gained_rope_tables(cos_t, sin_t, gq))
    rope_k = _gained_rope_tables(cos_t, sin_t, _deinterleave(attn_k_norm[0], 1))
    small_logits = _logit_bound(attn_q_norm[0], attn_k_norm[0]) <= MAX_UNSHIFTED_LOGIT
    params = (norm_g, w_t, w_kg, rope_q, rope_k, small_logits, attn_w_out[0].astype(BF16),
              pool_w_in[0].astype(BF16), pool_w_group[0].astype(BF16),
              pool_scale[0].reshape(1, POOL_WIDTH), pool_w_out[0].astype(BF16))

    y_prompt = _trunk(x_prompt, mod[:, :, :bp], params)
    y_sample = _trunk(x_sample, mod[:, :, bp:bp + bs], params)
    return (y_prompt, y_sample)
```

```python
import functools

import jax
import jax.numpy as jnp
from jax import lax
from jax.experimental import pallas as pl
from jax.experimental.pallas import tpu as pltpu

D_MODEL = 1024
DEPTH = 2
N_HEADS = 8
N_KV_HEADS = 2
HEAD_DIM = 128
GROUP = N_HEADS // N_KV_HEADS
Q_DIM = N_HEADS * HEAD_DIM
KV_DIM = N_KV_HEADS * HEAD_DIM
ATTN_IN = 2 * Q_DIM + 2 * KV_DIM
AXIS_DIM = HEAD_DIM // 2
ROPE_THETA = 10000.0
GRID_W = 64
POOL_WINDOWS = (2, 4, 8, 16)
N_POOL_GROUPS = len(POOL_WINDOWS)
POOL_WIDTH = D_MODEL
POOL_GROUP_WIDTH = POOL_WIDTH // N_POOL_GROUPS
RMS_EPS = 1e-6
LOG2_E = 1.4426950408889634
MAX_UNSHIFTED_LOGIT = 40.0

V7X_SUBLANES = 8
V7X_LANES = 128
MXU_WIDTH = 256
V7X_VMEM_BYTES = 64 * 1024 * 1024
VMEM_LIMIT_BYTES = (V7X_VMEM_BYTES * 3) // 4

ROW_TILE = 512
Q_TILE = 1024
KV_TILE = ROW_TILE
POOL_HALO = max(POOL_WINDOWS) // 2
HALO_BLOCK = 16

BF16 = jnp.bfloat16
F32 = jnp.float32
NT_DIMS = (((1,), (1,)), ((), ()))


def _compiler_params(n_grid_axes):
    return pltpu.CompilerParams(
        dimension_semantics=("arbitrary",) * n_grid_axes,
        vmem_limit_bytes=VMEM_LIMIT_BYTES,
    )


def _silu(x):
    return x * (1.0 / (1.0 + jnp.exp(-x)))


def _modulated_rmsnorm(x, norm_g, scl, shift):
    ms = jnp.mean(x * x, axis=-1, keepdims=True)
    return (x * lax.rsqrt(ms + RMS_EPS)) * norm_g * (1.0 + scl) + shift


def _mod_kernel(c_ref, w_ref, b_ref, o_ref):
    cs = _silu(c_ref[...])
    w = w_ref[0]
    cs_hi = cs.astype(BF16)
    cs_lo = (cs - cs_hi.astype(F32)).astype(BF16)
    w_hi = w.astype(BF16)
    w_lo = (w - w_hi.astype(F32)).astype(BF16)
    dot = functools.partial(jnp.dot, preferred_element_type=F32)
    o_ref[0, 0] = dot(cs_hi, w_hi) + (dot(cs_lo, w_hi) + dot(cs_hi, w_lo)) + b_ref[0, 0]


def _modulation(c_pad, ada_w, ada_b):
    bp = c_pad.shape[0]
    return pl.pallas_call(
        _mod_kernel,
        grid=(DEPTH, 3),
        in_specs=[
            pl.BlockSpec((bp, D_MODEL), lambda i, j: (0, 0)),
            pl.BlockSpec((1, D_MODEL, D_MODEL), lambda i, j: (i, 0, j)),
            pl.BlockSpec((1, 1, 1, D_MODEL), lambda i, j: (i, j, 0, 0)),
        ],
        out_specs=pl.BlockSpec((1, 1, bp, D_MODEL), lambda i, j: (i, j, 0, 0)),
        out_shape=jax.ShapeDtypeStruct((DEPTH, 3, bp, D_MODEL), F32),
        compiler_params=_compiler_params(2),
        name="adaln_modulation",
    )(c_pad, ada_w, ada_b.reshape(DEPTH, 3, 1, D_MODEL))


def _attn_in_kernel(x_ref, mod_ref, ng_ref, wt_ref, w_ref, qa_ref, qb_ref, ka_ref, kb_ref,
                    qt_ref, k_ref, vt_ref, sg_ref):
    h = _modulated_rmsnorm(x_ref[0], ng_ref[...], mod_ref[1, 0], mod_ref[0, 0]).astype(BF16)
    yt = lax.dot_general(wt_ref[...], h, NT_DIMS, preferred_element_type=F32)
    y = jnp.dot(h, w_ref[...], preferred_element_type=F32)
    half = HEAD_DIM // 2
    qa = qa_ref[...]
    qb = qb_ref[...]
    for hh in range(N_HEADS):
        t = yt[hh * HEAD_DIM:(hh + 1) * HEAD_DIM]
        rs = lax.rsqrt(jnp.mean(t * t, axis=0, keepdims=True) + RMS_EPS)
        partner = jnp.concatenate([t[half:], t[:half]], axis=0)
        qt_ref[0, hh] = ((t * qa + partner * qb) * rs).astype(BF16)
    for j in range(N_KV_HEADS):
        lo = Q_DIM + j * HEAD_DIM
        vt_ref[0, j, 0] = yt[lo:lo + HEAD_DIM].astype(BF16)
        t = y[:, j * HEAD_DIM:(j + 1) * HEAD_DIM]
        rs = lax.rsqrt(jnp.mean(t * t, axis=1, keepdims=True) + RMS_EPS)
        k_ref[0, j] = ((t * ka_ref[...] + pltpu.roll(t, half, 1) * kb_ref[...]) * rs).astype(BF16)
    sg_ref[0] = _silu(y[:, KV_DIM:]).astype(BF16)


def _attn_in_proj(x, mod, norm_g, w_t, w_kg, rope_q, rope_k):
    b, s, _ = x.shape
    tm = ROW_TILE
    const = lambda *_: (0, 0)
    return pl.pallas_call(
        _attn_in_kernel,
        grid=(b, s // tm),
        in_specs=[
            pl.BlockSpec((1, tm, D_MODEL), lambda bi, i: (bi, i, 0)),
            pl.BlockSpec((3, 1, 1, D_MODEL), lambda bi, i: (0, bi, 0, 0)),
            pl.BlockSpec((1, D_MODEL), const),
            pl.BlockSpec((Q_DIM + KV_DIM, D_MODEL), const),
            pl.BlockSpec((D_MODEL, KV_DIM + Q_DIM), const),
            pl.BlockSpec((HEAD_DIM, tm), lambda bi, i: (0, i)),
            pl.BlockSpec((HEAD_DIM, tm), lambda bi, i: (0, i)),
            pl.BlockSpec((tm, HEAD_DIM), lambda bi, i: (i, 0)),
            pl.BlockSpec((tm, HEAD_DIM), lambda bi, i: (i, 0)),
        ],
        out_specs=[
            pl.BlockSpec((1, N_HEADS, HEAD_DIM, tm), lambda bi, i: (bi, 0, 0, i)),
            pl.BlockSpec((1, N_KV_HEADS, tm, HEAD_DIM), lambda bi, i: (bi, 0, i, 0)),
            pl.BlockSpec((1, N_KV_HEADS, 1, HEAD_DIM, tm), lambda bi, i: (bi, 0, i, 0, 0)),
            pl.BlockSpec((1, tm, Q_DIM), lambda bi, i: (bi, i, 0)),
        ],
        out_shape=[
            jax.ShapeDtypeStruct((b, N_HEADS, HEAD_DIM, s), BF16),
            jax.ShapeDtypeStruct((b, N_KV_HEADS, s, HEAD_DIM), BF16),
            jax.ShapeDtypeStruct((b, N_KV_HEADS, s // tm, HEAD_DIM, tm), BF16),
            jax.ShapeDtypeStruct((b, s, Q_DIM), BF16),
        ],
        compiler_params=_compiler_params(2),
        name="attn_in_proj",
    )(x, mod, norm_g, w_t, w_kg, *rope_q, *rope_k)


def _flash_kernel(qt_ref, k_ref, vt_ref, o_ref, *scratch, n_kv, tk, running_max):
    q_t = qt_ref[0, 0]
    tq = q_t.shape[1]

    def scores_t(j):
        return jnp.dot(k_ref[0, 0, pl.ds(j * tk, tk), :], q_t, preferred_element_type=F32)

    if running_max:
        m_sc, l_sc, acc_sc = scratch
        m_sc[...] = jnp.full(m_sc.shape, -jnp.inf, F32)
        l_sc[...] = jnp.zeros(l_sc.shape, F32)
        acc_sc[...] = jnp.zeros(acc_sc.shape, F32)

        def body(j, carry):
            s_t = scores_t(j)
            m_prev = m_sc[...]
            m_next = jnp.maximum(m_prev, jnp.max(s_t, axis=0, keepdims=True))
            alpha = jnp.exp2(m_prev - m_next)
            p_t = jnp.exp2(s_t - m_next)
            l_sc[...] = alpha * l_sc[...] + jnp.sum(p_t, axis=0, keepdims=True)
            acc_sc[...] = acc_sc[...] * alpha + jnp.dot(
                vt_ref[0, 0, j], p_t.astype(BF16), preferred_element_type=F32)
            m_sc[...] = m_next
            return carry

        lax.fori_loop(0, n_kv, body, 0)
        o_t = acc_sc[...] / l_sc[...]
    else:
        acc = jnp.zeros((HEAD_DIM, tq), F32)
        l_part = jnp.zeros((V7X_SUBLANES, tq), F32)
        s_next = scores_t(0)
        for j in range(n_kv):
            s_t = s_next
            if j + 1 < n_kv:
                s_next = scores_t(j + 1)
            p_t = jnp.exp2(s_t)
            l_part = l_part + jnp.sum(p_t.reshape(tk // V7X_SUBLANES, V7X_SUBLANES, tq), axis=0)
            acc = acc + jnp.dot(vt_ref[0, 0, j], p_t.astype(BF16), preferred_element_type=F32)
        o_t = acc / jnp.sum(l_part, axis=0, keepdims=True)
    o_ref[0, 0] = o_t.T.astype(BF16)


def _flash_attention(qt, k, vt, *, running_max):
    b, nh, dh, s = qt.shape
    nkv, n_kv, tk = vt.shape[1], vt.shape[2], vt.shape[4]
    group = nh // nkv
    tq = Q_TILE
    n_q = s // tq
    kern = functools.partial(_flash_kernel, n_kv=n_kv, tk=tk, running_max=running_max)
    stat = pltpu.VMEM((1, tq), F32)
    return pl.pallas_call(
        kern,
        grid=(b, nkv, group * n_q),
        in_specs=[
            pl.BlockSpec((1, 1, dh, tq), lambda bi, g, i: (bi, g * group + i // n_q, 0, i % n_q)),
            pl.BlockSpec((1, 1, s, dh), lambda bi, g, i: (bi, g, 0, 0)),
            pl.BlockSpec((1, 1, n_kv, dh, tk), lambda bi, g, i: (bi, g, 0, 0, 0)),
        ],
        out_specs=pl.BlockSpec((1, 1, tq, dh), lambda bi, g, i: (bi, g * group + i // n_q, i % n_q, 0)),
        out_shape=jax.ShapeDtypeStruct((b, nh, s, dh), BF16),
        scratch_shapes=[stat, stat, pltpu.VMEM((dh, tq), F32)] if running_max else [],
        compiler_params=_compiler_params(3),
        name="flash_attention_online" if running_max else "flash_attention",
    )(qt, k, vt)


def _out_pool_kernel(op_ref, o_ref, on_ref, sgp_ref, sg_ref, sgn_ref, xp_ref, x_ref, xn_ref,
                     mod0_ref, mod1_ref, wao_ref, ng_ref, win_ref, wg_ref, sc_ref, wout_ref, y_ref,
                     *, tm, seq):
    blk = HALO_BLOCK
    rows = tm + 2 * blk
    i = pl.program_id(1)
    n_tiles = pl.num_programs(1)

    def with_halo(prev, main, nxt):
        return jnp.concatenate([prev, main, nxt], axis=0)

    heads = lambda ref: jnp.concatenate([ref[0, hh] for hh in range(N_HEADS)], axis=1)
    o = with_halo(heads(op_ref), heads(o_ref), heads(on_ref))
    sg = with_halo(sgp_ref[0], sg_ref[0], sgn_ref[0])
    x0 = with_halo(xp_ref[0], x_ref[0], xn_ref[0])
    x1 = x0 + mod0_ref[2, 0] * jnp.dot(o * sg, wao_ref[...], preferred_element_type=F32)
    x = x1[blk:blk + tm]

    h = _modulated_rmsnorm(x1, ng_ref[...], mod1_ref[1, 0], mod1_ref[0, 0])
    y = jnp.dot(h.astype(BF16), win_ref[...], preferred_element_type=F32)
    u = y[:, :POOL_WIDTH]
    u = jnp.concatenate([jnp.where(i > 0, u[:blk], 0.0),
                         u[blk:blk + tm],
                         jnp.where(i < n_tiles - 1, u[blk + tm:], 0.0)], axis=0)
    gate = y[blk:blk + tm, POOL_WIDTH:]
    t = lax.broadcasted_iota(jnp.int32, (tm, 1), 0) + i * tm

    ahead = lambda a, d: pltpu.roll(a, rows - d, 0)
    behind = lambda a, d: pltpu.roll(a, d, 0)

    mixed = []
    for j, w in enumerate(POOL_WINDOWS):
        ug = u[:, j * POOL_GROUP_WIDTH:(j + 1) * POOL_GROUP_WIDTH]
        run, span = ug, 1
        while 2 * span < w:
            run, span = run + ahead(run, span), 2 * span
        win = behind(run, span) + run
        cnt = jnp.minimum(t + w // 2, seq) - jnp.maximum(t - w // 2, 0)
        pooled = win[blk:blk + tm] * (1.0 / cnt.astype(F32))
        mix = (pooled - ug[blk:blk + tm]).astype(BF16)
        mixed.append(jnp.dot(mix, wg_ref[j], preferred_element_type=F32))
    z = jnp.concatenate(mixed, axis=1) * sc_ref[...] * _silu(gate)
    out = jnp.dot(z.astype(BF16), wout_ref[...], preferred_element_type=F32)
    y_ref[0] = x + mod1_ref[2, 0] * out


def _out_proj_and_pool_layer(o, sg, x, mod0, mod1, w_attn_out, norm_g, w_in, w_group, scale, w_out):
    b, s, _ = x.shape
    tm = ROW_TILE
    blk = HALO_BLOCK
    per = tm // blk
    last = s // blk - 1
    prev = lambda i: jnp.maximum(i * per - 1, 0)
    nxt = lambda i: jnp.minimum((i + 1) * per, last)
    const2 = lambda *_: (0, 0)
    mod_spec = pl.BlockSpec((3, 1, 1, D_MODEL), lambda bi, i: (0, bi, 0, 0))

    def rows_specs(width):
        return [pl.BlockSpec((1, blk, width), lambda bi, i: (bi, prev(i), 0)),
                pl.BlockSpec((1, tm, width), lambda bi, i: (bi, i, 0)),
                pl.BlockSpec((1, blk, width), lambda bi, i: (bi, nxt(i), 0))]

    kern = functools.partial(_out_pool_kernel, tm=tm, seq=s)
    return pl.pallas_call(
        kern,
        grid=(b, s // tm),
        in_specs=[
            pl.BlockSpec((1, N_HEADS, blk, HEAD_DIM), lambda bi, i: (bi, 0, prev(i), 0)),
            pl.BlockSpec((1, N_HEADS, tm, HEAD_DIM), lambda bi, i: (bi, 0, i, 0)),
            pl.BlockSpec((1, N_HEADS, blk, HEAD_DIM), lambda bi, i: (bi, 0, nxt(i), 0)),
            *rows_specs(Q_DIM),
            *rows_specs(D_MODEL),
            mod_spec,
            mod_spec,
            pl.BlockSpec((Q_DIM, D_MODEL), const2),
            pl.BlockSpec((1, D_MODEL), const2),
            pl.BlockSpec((D_MODEL, 2 * POOL_WIDTH), const2),
            pl.BlockSpec((N_POOL_GROUPS, POOL_GROUP_WIDTH, POOL_GROUP_WIDTH), lambda *_: (0, 0, 0)),
            pl.BlockSpec((1, POOL_WIDTH), const2),
            pl.BlockSpec((POOL_WIDTH, D_MODEL), const2),
        ],
        out_specs=pl.BlockSpec((1, tm, D_MODEL), lambda bi, i: (bi, i, 0)),
        out_shape=jax.ShapeDtypeStruct((b, s, D_MODEL), F32),
        compiler_params=_compiler_params(2),
        name="out_proj_pool_layer",
    )(o, o, o, sg, sg, sg, x, x, x, mod0, mod1, w_attn_out, norm_g, w_in, w_group, scale, w_out)


def _rope_tables(s):
    rows = s // GRID_W
    row = jnp.broadcast_to(jnp.arange(rows)[:, None], (rows, GRID_W)).reshape(s).astype(F32)
    col = jnp.broadcast_to(jnp.arange(GRID_W)[None, :], (rows, GRID_W)).reshape(s).astype(F32)
    inv = ROPE_THETA ** (-jnp.arange(0, AXIS_DIM, 2, dtype=F32) / AXIS_DIM)
    ang = jnp.concatenate([row[:, None] * inv, col[:, None] * inv], axis=-1)
    c, sn = jnp.cos(ang), jnp.sin(ang)
    return jnp.concatenate([c, c], axis=-1), jnp.concatenate([-sn, sn], axis=-1)


def _gained_rope_tables(cos_t, sin_t, gain):
    half = HEAD_DIM // 2
    partner_gain = jnp.concatenate([gain[half:], gain[:half]])
    return cos_t * gain, sin_t * partner_gain


def _logit_bound(q_gain, k_gain):
    return (HEAD_DIM ** 0.5 * 1.01) * jnp.max(jnp.abs(q_gain)) * jnp.max(jnp.abs(k_gain))


def _deinterleave(a, n_heads):
    lead = a.shape[:-1]
    a = a.reshape(lead + (n_heads, HEAD_DIM // 2, 2))
    a = jnp.concatenate([a[..., 0], a[..., 1]], axis=-1)
    return a.reshape(lead + (n_heads * HEAD_DIM,))


def _trunk(x, mod, params):
    (norm_g, w_t, w_kg, rope_q, rope_k, small_logits, attn_w_out,
     pool_w_in, pool_w_group, pool_scale, pool_w_out) = params
    qt, k, vt, sg = _attn_in_proj(x, mod[0], norm_g[0:1], w_t, w_kg, rope_q, rope_k)
    o = lax.cond(small_logits,
                 functools.partial(_flash_attention, running_max=False),
                 functools.partial(_flash_attention, running_max=True),
                 qt, k, vt)
    return _out_proj_and_pool_layer(o, sg, x, mod[0], mod[1], attn_w_out, norm_g[1:2],
                                    pool_w_in, pool_w_group, pool_scale, pool_w_out)


def kernel(x_prompt, x_sample, c_prompt, c_sample, norm_g, ada_w, ada_b, attn_w_in, attn_q_norm,
           attn_k_norm, attn_w_out, pool_w_in, pool_w_group, pool_scale, pool_w_out):
    assert DEPTH == 2 and attn_w_in.shape[0] == 1 and pool_w_in.shape[0] == 1
    bp, bs = x_prompt.shape[0], x_sample.shape[0]
    s = x_prompt.shape[1]
    assert x_sample.shape[1] == s and s % ROW_TILE == 0 and s % KV_TILE == 0

    c_all = jnp.concatenate([c_prompt, c_sample], axis=0)
    pad = (-c_all.shape[0]) % V7X_SUBLANES
    c_pad = jnp.pad(c_all, ((0, pad), (0, 0)))
    mod = _modulation(c_pad, ada_w, ada_b)
    mod = mod.reshape(DEPTH, 3, c_pad.shape[0], 1, D_MODEL)

    w_in = attn_w_in[0]
    w_q = _deinterleave(w_in[:, :Q_DIM], N_HEADS)
    w_k = _deinterleave(w_in[:, Q_DIM:Q_DIM + KV_DIM], N_KV_HEADS)
    w_v = w_in[:, Q_DIM + KV_DIM:Q_DIM + 2 * KV_DIM]
    w_t = jnp.concatenate([w_q, w_v], axis=1).T.astype(BF16)
    w_kg = jnp.concatenate([w_k, w_in[:, Q_DIM + 2 * KV_DIM:]], axis=1).astype(BF16)
    cos_t, sin_t = _rope_tables(s)
    gq = _deinterleave(attn_q_norm[0], 1) * (HEAD_DIM ** -0.5 * LOG2_E)
    rope_q = tuple(t.T for t in _gained_rope_tables(cos_t, sin_t, gq))
    rope_k = _gained_rope_tables(cos_t, sin_t, _deinterleave(attn_k_norm[0], 1))
    small_logits = _logit_bound(attn_q_norm[0], attn_k_norm[0]) <= MAX_UNSHIFTED_LOGIT
    params = (norm_g, w_t, w_kg, rope_q, rope_k, small_logits, attn_w_out[0].astype(BF16),
              pool_w_in[0].astype(BF16), pool_w_group[0].astype(BF16),
              pool_scale[0].reshape(1, POOL_WIDTH), pool_w_out[0].astype(BF16))

    y_prompt = _trunk(x_prompt, mod[:, :, :bp], params)
    y_sample = _trunk(x_sample, mod[:, :, bp:bp + bs], params)
    return (y_prompt, y_sample)
```

```python
import functools

import jax
import jax.numpy as jnp
from jax import lax
from jax.experimental import pallas as pl
from jax.experimental.pallas import tpu as pltpu

D_MODEL = 1024
DEPTH = 2
N_HEADS = 8
N_KV_HEADS = 2
HEAD_DIM = 128
GROUP = N_HEADS // N_KV_HEADS
Q_DIM = N_HEADS * HEAD_DIM
KV_DIM = N_KV_HEADS * HEAD_DIM
ATTN_IN = 2 * Q_DIM + 2 * KV_DIM
AXIS_DIM = HEAD_DIM // 2
ROPE_THETA = 10000.0
GRID_W = 64
POOL_WINDOWS = (2, 4, 8, 16)
N_POOL_GROUPS = len(POOL_WINDOWS)
POOL_WIDTH = D_MODEL
POOL_GROUP_WIDTH = POOL_WIDTH // N_POOL_GROUPS
RMS_EPS = 1e-6
LOG2_E = 1.4426950408889634
MAX_UNSHIFTED_LOGIT = 40.0

V7X_SUBLANES = 8
V7X_LANES = 128
MXU_WIDTH = 256
V7X_VMEM_BYTES = 64 * 1024 * 1024
VMEM_LIMIT_BYTES = (V7X_VMEM_BYTES * 3) // 4

ROW_TILE = 512
IN_PROJ_ROW_SLABS = 2
Q_TILE = 2048
KV_TILE = ROW_TILE
POOL_HALO = max(POOL_WINDOWS) // 2
HALO_BLOCK = 16

BF16 = jnp.bfloat16
F32 = jnp.float32
NT_DIMS = (((1,), (1,)), ((), ()))


def _compiler_params(n_grid_axes):
    return pltpu.CompilerParams(
        dimension_semantics=("arbitrary",) * n_grid_axes,
        vmem_limit_bytes=VMEM_LIMIT_BYTES,
    )


def _silu(x):
    return x * (1.0 / (1.0 + jnp.exp(-x)))


def _modulated_rmsnorm(x, norm_g, scl, shift):
    ms = jnp.mean(x * x, axis=-1, keepdims=True)
    return (x * lax.rsqrt(ms + RMS_EPS)) * norm_g * (1.0 + scl) + shift


def _mod_kernel(c_ref, w_ref, b_ref, o_ref):
    cs = _silu(c_ref[...])
    w = w_ref[0]
    cs_hi = cs.astype(BF16)
    cs_lo = (cs - cs_hi.astype(F32)).astype(BF16)
    w_hi = w.astype(BF16)
    w_lo = (w - w_hi.astype(F32)).astype(BF16)
    dot = functools.partial(jnp.dot, preferred_element_type=F32)
    o_ref[0, 0] = dot(cs_hi, w_hi) + (dot(cs_lo, w_hi) + dot(cs_hi, w_lo)) + b_ref[0, 0]


def _modulation(c_pad, ada_w, ada_b):
    bp = c_pad.shape[0]
    return pl.pallas_call(
        _mod_kernel,
        grid=(DEPTH, 3),
        in_specs=[
            pl.BlockSpec((bp, D_MODEL), lambda i, j: (0, 0)),
            pl.BlockSpec((1, D_MODEL, D_MODEL), lambda i, j: (i, 0, j)),
            pl.BlockSpec((1, 1, 1, D_MODEL), lambda i, j: (i, j, 0, 0)),
        ],
        out_specs=pl.BlockSpec((1, 1, bp, D_MODEL), lambda i, j: (i, j, 0, 0)),
        out_shape=jax.ShapeDtypeStruct((DEPTH, 3, bp, D_MODEL), F32),
        compiler_params=_compiler_params(2),
        name="adaln_modulation",
    )(c_pad, ada_w, ada_b.reshape(DEPTH, 3, 1, D_MODEL))


def _attn_in_kernel(x_ref, mod_ref, ng_ref, wt_ref, w_ref, qa_ref, qb_ref, ka_ref, kb_ref,
                    qt_ref, k_ref, vt_ref, sg_ref):
    tm = x_ref.shape[1]
    half = HEAD_DIM // 2
    slab = tm // IN_PROJ_ROW_SLABS
    hs, ys = [], []
    for r in range(IN_PROJ_ROW_SLABS):
        rows = pl.ds(r * slab, slab)
        hs.append(_modulated_rmsnorm(x_ref[0, rows], ng_ref[...], mod_ref[1, 0], mod_ref[0, 0]).astype(BF16))
        ys.append(jnp.dot(hs[-1], w_ref[...], preferred_element_type=F32))
    h = jnp.concatenate(hs, axis=0)
    yt = lax.dot_general(wt_ref[...], h, NT_DIMS, preferred_element_type=F32)
    for r, y in enumerate(ys):
        rows = pl.ds(r * slab, slab)
        for j in range(N_KV_HEADS):
            t = y[:, j * HEAD_DIM:(j + 1) * HEAD_DIM]
            rs = lax.rsqrt(jnp.mean(t * t, axis=1, keepdims=True) + RMS_EPS)
            k_ref[0, j, rows] = ((t * ka_ref[rows] + pltpu.roll(t, half, 1) * kb_ref[rows]) * rs).astype(BF16)
        sg_ref[0, rows] = _silu(y[:, KV_DIM:]).astype(BF16)
    qa = qa_ref[...]
    qb = qb_ref[...]
    for hh in range(N_HEADS):
        t = yt[hh * HEAD_DIM:(hh + 1) * HEAD_DIM]
        rs = lax.rsqrt(jnp.mean(t * t, axis=0, keepdims=True) + RMS_EPS)
        partner = jnp.concatenate([t[half:], t[:half]], axis=0)
        qt_ref[0, hh] = ((t * qa + partner * qb) * rs).astype(BF16)
    for j in range(N_KV_HEADS):
        lo = Q_DIM + j * HEAD_DIM
        vt_ref[0, j, 0] = yt[lo:lo + HEAD_DIM].astype(BF16)


def _attn_in_proj(x, mod, norm_g, w_t, w_kg, rope_q, rope_k):
    b, s, _ = x.shape
    tm = ROW_TILE
    const = lambda *_: (0, 0)
    return pl.pallas_call(
        _attn_in_kernel,
        grid=(b, s // tm),
        in_specs=[
            pl.BlockSpec((1, tm, D_MODEL), lambda bi, i: (bi, i, 0)),
            pl.BlockSpec((3, 1, 1, D_MODEL), lambda bi, i: (0, bi, 0, 0)),
            pl.BlockSpec((1, D_MODEL), const),
            pl.BlockSpec((Q_DIM + KV_DIM, D_MODEL), const),
            pl.BlockSpec((D_MODEL, KV_DIM + Q_DIM), const),
            pl.BlockSpec((HEAD_DIM, tm), lambda bi, i: (0, i)),
            pl.BlockSpec((HEAD_DIM, tm), lambda bi, i: (0, i)),
            pl.BlockSpec((tm, HEAD_DIM), lambda bi, i: (i, 0)),
            pl.BlockSpec((tm, HEAD_DIM), lambda bi, i: (i, 0)),
        ],
        out_specs=[
            pl.BlockSpec((1, N_HEADS, HEAD_DIM, tm), lambda bi, i: (bi, 0, 0, i)),
            pl.BlockSpec((1, N_KV_HEADS, tm, HEAD_DIM), lambda bi, i: (bi, 0, i, 0)),
            pl.BlockSpec((1, N_KV_HEADS, 1, HEAD_DIM, tm), lambda bi, i: (bi, 0, i, 0, 0)),
            pl.BlockSpec((1, tm, Q_DIM), lambda bi, i: (bi, i, 0)),
        ],
        out_shape=[
            jax.ShapeDtypeStruct((b, N_HEADS, HEAD_DIM, s), BF16),
            jax.ShapeDtypeStruct((b, N_KV_HEADS, s, HEAD_DIM), BF16),
            jax.ShapeDtypeStruct((b, N_KV_HEADS, s // tm, HEAD_DIM, tm), BF16),
            jax.ShapeDtypeStruct((b, s, Q_DIM), BF16),
        ],
        compiler_params=_compiler_params(2),
        name="attn_in_proj",
    )(x, mod, norm_g, w_t, w_kg, *rope_q, *rope_k)


def _flash_kernel(qt_ref, k_ref, vt_ref, o_ref, *scratch, n_kv, tk, running_max):
    q_t = qt_ref[0, 0]
    tq = q_t.shape[1]

    def scores_t(j):
        return jnp.dot(k_ref[0, 0, pl.ds(j * tk, tk), :], q_t, preferred_element_type=F32)

    if running_max:
        m_sc, l_sc, acc_sc = scratch
        m_sc[...] = jnp.full(m_sc.shape, -jnp.inf, F32)
        l_sc[...] = jnp.zeros(l_sc.shape, F32)
        acc_sc[...] = jnp.zeros(acc_sc.shape, F32)

        def body(j, carry):
            s_t = scores_t(j)
            m_prev = m_sc[...]
            m_next = jnp.maximum(m_prev, jnp.max(s_t, axis=0, keepdims=True))
            alpha = jnp.exp2(m_prev - m_next)
            p_t = jnp.exp2(s_t - m_next)
            l_sc[...] = alpha * l_sc[...] + jnp.sum(p_t, axis=0, keepdims=True)
            acc_sc[...] = acc_sc[...] * alpha + jnp.dot(
                vt_ref[0, 0, j], p_t.astype(BF16), preferred_element_type=F32)
            m_sc[...] = m_next
            return carry

        lax.fori_loop(0, n_kv, body, 0)
        o_t = acc_sc[...] / l_sc[...]
    else:
        acc = jnp.zeros((HEAD_DIM, tq), F32)
        l_part = jnp.zeros((V7X_SUBLANES, tq), F32)
        s_next = scores_t(0)
        for j in range(n_kv):
            s_t = s_next
            if j + 1 < n_kv:
                s_next = scores_t(j + 1)
            p_t = jnp.exp2(s_t)
            l_part = l_part + jnp.sum(p_t.reshape(tk // V7X_SUBLANES, V7X_SUBLANES, tq), axis=0)
            acc = acc + jnp.dot(vt_ref[0, 0, j], p_t.astype(BF16), preferred_element_type=F32)
        o_t = acc / jnp.sum(l_part, axis=0, keepdims=True)
    o_ref[0, 0] = o_t.T.astype(BF16)


def _flash_attention(qt, k, vt, *, running_max):
    b, nh, dh, s = qt.shape
    nkv, n_kv, tk = vt.shape[1], vt.shape[2], vt.shape[4]
    group = nh // nkv
    tq = min(Q_TILE, s)
    assert s % tq == 0
    n_q = s // tq
    kern = functools.partial(_flash_kernel, n_kv=n_kv, tk=tk, running_max=running_max)
    stat = pltpu.VMEM((1, tq), F32)
    return pl.pallas_call(
        kern,
        grid=(b, nkv, group * n_q),
        in_specs=[
            pl.BlockSpec((1, 1, dh, tq), lambda bi, g, i: (bi, g * group + i // n_q, 0, i % n_q)),
            pl.BlockSpec((1, 1, s, dh), lambda bi, g, i: (bi, g, 0, 0)),
            pl.BlockSpec((1, 1, n_kv, dh, tk), lambda bi, g, i: (bi, g, 0, 0, 0)),
        ],
        out_specs=pl.BlockSpec((1, 1, tq, dh), lambda bi, g, i: (bi, g * group + i // n_q, i % n_q, 0)),
        out_shape=jax.ShapeDtypeStruct((b, nh, s, dh), BF16),
        scratch_shapes=[stat, stat, pltpu.VMEM((dh, tq), F32)] if running_max else [],
        compiler_params=_compiler_params(3),
        name="flash_attention_online" if running_max else "flash_attention",
    )(qt, k, vt)


def _out_pool_kernel(op_ref, o_ref, on_ref, sgp_ref, sg_ref, sgn_ref, xp_ref, x_ref, xn_ref,
                     mod0_ref, mod1_ref, wao_ref, ng_ref, win_ref, wg_ref, sc_ref, wout_ref, y_ref,
                     *, tm, seq):
    blk = HALO_BLOCK
    rows = tm + 2 * blk
    i = pl.program_id(1)
    n_tiles = pl.num_programs(1)

    def with_halo(prev, main, nxt):
        return jnp.concatenate([prev, main, nxt], axis=0)

    heads = lambda ref: jnp.concatenate([ref[0, hh] for hh in range(N_HEADS)], axis=1)
    o = with_halo(heads(op_ref), heads(o_ref), heads(on_ref))
    sg = with_halo(sgp_ref[0], sg_ref[0], sgn_ref[0])
    x0 = with_halo(xp_ref[0], x_ref[0], xn_ref[0])
    x1 = x0 + mod0_ref[2, 0] * jnp.dot(o * sg, wao_ref[...], preferred_element_type=F32)
    x = x1[blk:blk + tm]

    h = _modulated_rmsnorm(x1, ng_ref[...], mod1_ref[1, 0], mod1_ref[0, 0])
    y = jnp.dot(h.astype(BF16), win_ref[...], preferred_element_type=F32)
    u = y[:, :POOL_WIDTH]
    u = jnp.concatenate([jnp.where(i > 0, u[:blk], 0.0),
                         u[blk:blk + tm],
                         jnp.where(i < n_tiles - 1, u[blk + tm:], 0.0)], axis=0)
    gate = y[blk:blk + tm, POOL_WIDTH:]
    t = lax.broadcasted_iota(jnp.int32, (tm, 1), 0) + i * tm

    ahead = lambda a, d: pltpu.roll(a, rows - d, 0)
    behind = lambda a, d: pltpu.roll(a, d, 0)

    mixed = []
    for j, w in enumerate(POOL_WINDOWS):
        ug = u[:, j * POOL_GROUP_WIDTH:(j + 1) * POOL_GROUP_WIDTH]
        run, span = ug, 1
        while 2 * span < w:
            run, span = run + ahead(run, span), 2 * span
        win = behind(run, span) + run
        cnt = jnp.minimum(t + w // 2, seq) - jnp.maximum(t - w // 2, 0)
        pooled = win[blk:blk + tm] * (1.0 / cnt.astype(F32))
        mix = (pooled - ug[blk:blk + tm]).astype(BF16)
        mixed.append(jnp.dot(mix, wg_ref[j], preferred_element_type=F32))
    z = jnp.concatenate(mixed, axis=1) * sc_ref[...] * _silu(gate)
    out = jnp.dot(z.astype(BF16), wout_ref[...], preferred_element_type=F32)
    y_ref[0] = x + mod1_ref[2, 0] * out


def _out_proj_and_pool_layer(o, sg, x, mod0, mod1, w_attn_out, norm_g, w_in, w_group, scale, w_out):
    b, s, _ = x.shape
    tm = ROW_TILE
    blk = HALO_BLOCK
    per = tm // blk
    last = s // blk - 1
    prev = lambda i: jnp.maximum(i * per - 1, 0)
    nxt = lambda i: jnp.minimum((i + 1) * per, last)
    const2 = lambda *_: (0, 0)
    mod_spec = pl.BlockSpec((3, 1, 1, D_MODEL), lambda bi, i: (0, bi, 0, 0))

    def rows_specs(width):
        return [pl.BlockSpec((1, blk, width), lambda bi, i: (bi, prev(i), 0)),
                pl.BlockSpec((1, tm, width), lambda bi, i: (bi, i, 0)),
                pl.BlockSpec((1, blk, width), lambda bi, i: (bi, nxt(i), 0))]

    kern = functools.partial(_out_pool_kernel, tm=tm, seq=s)
    return pl.pallas_call(
        kern,
        grid=(b, s // tm),
        in_specs=[
            pl.BlockSpec((1, N_HEADS, blk, HEAD_DIM), lambda bi, i: (bi, 0, prev(i), 0)),
            pl.BlockSpec((1, N_HEADS, tm, HEAD_DIM), lambda bi, i: (bi, 0, i, 0)),
            pl.BlockSpec((1, N_HEADS, blk, HEAD_DIM), lambda bi, i: (bi, 0, nxt(i), 0)),
            *rows_specs(Q_DIM),
            *rows_specs(D_MODEL),
            mod_spec,
            mod_spec,
            pl.BlockSpec((Q_DIM, D_MODEL), const2),
            pl.BlockSpec((1, D_MODEL), const2),
            pl.BlockSpec((D_MODEL, 2 * POOL_WIDTH), const2),
            pl.BlockSpec((N_POOL_GROUPS, POOL_GROUP_WIDTH, POOL_GROUP_WIDTH), lambda *_: (0, 0, 0)),
            pl.BlockSpec((1, POOL_WIDTH), const2),
            pl.BlockSpec((POOL_WIDTH, D_MODEL), const2),
        ],
        out_specs=pl.BlockSpec((1, tm, D_MODEL), lambda bi, i: (bi, i, 0)),
        out_shape=jax.ShapeDtypeStruct((b, s, D_MODEL), F32),
        compiler_params=_compiler_params(2),
        name="out_proj_pool_layer",
    )(o, o, o, sg, sg, sg, x, x, x, mod0, mod1, w_attn_out, norm_g, w_in, w_group, scale, w_out)


def _rope_tables(s):
    rows = s // GRID_W
    inv = ROPE_THETA ** (-jnp.arange(0, AXIS_DIM, 2, dtype=F32) / AXIS_DIM)
    n_freq = inv.shape[0]
    row_ang = jnp.arange(rows, dtype=F32)[:, None] * inv
    col_ang = jnp.arange(GRID_W, dtype=F32)[:, None] * inv

    def on_grid(fn):
        r = jnp.broadcast_to(fn(row_ang)[:, None, :], (rows, GRID_W, n_freq))
        c = jnp.broadcast_to(fn(col_ang)[None, :, :], (rows, GRID_W, n_freq))
        return jnp.concatenate([r, c], axis=-1).reshape(s, 2 * n_freq)

    c, sn = on_grid(jnp.cos), on_grid(jnp.sin)
    return jnp.concatenate([c, c], axis=-1), jnp.concatenate([-sn, sn], axis=-1)


def _gained_rope_tables(cos_t, sin_t, gain):
    half = HEAD_DIM // 2
    partner_gain = jnp.concatenate([gain[half:], gain[:half]])
    return cos_t * gain, sin_t * partner_gain


def _logit_bound(q_gain, k_gain):
    return (HEAD_DIM ** 0.5 * 1.01) * jnp.max(jnp.abs(q_gain)) * jnp.max(jnp.abs(k_gain))


def _deinterleave(a, n_heads):
    lead = a.shape[:-1]
    a = a.reshape(lead + (n_heads, HEAD_DIM // 2, 2))
    a = jnp.concatenate([a[..., 0], a[..., 1]], axis=-1)
    return a.reshape(lead + (n_heads * HEAD_DIM,))


def _trunk(x, mod, params):
    (norm_g, w_t, w_kg, rope_q, rope_k, small_logits, attn_w_out,
     pool_w_in, pool_w_group, pool_scale, pool_w_out) = params
    qt, k, vt, sg = _attn_in_proj(x, mod[0], norm_g[0:1], w_t, w_kg, rope_q, rope_k)
    o = lax.cond(small_logits,
                 functools.partial(_flash_attention, running_max=False),
                 functools.partial(_flash_attention, running_max=True),
                 qt, k, vt)
    return _out_proj_and_pool_layer(o, sg, x, mod[0], mod[1], attn_w_out, norm_g[1:2],
                                    pool_w_in, pool_w_group, pool_scale, pool_w_out)


def kernel(x_prompt, x_sample, c_prompt, c_sample, norm_g, ada_w, ada_b, attn_w_in, attn_q_norm,
           attn_k_norm, attn_w_out, pool_w_in, pool_w_group, pool_scale, pool_w_out):
    assert DEPTH == 2 and attn_w_in.shape[0] == 1 and pool_w_in.shape[0] == 1
    bp, bs = x_prompt.shape[0], x_sample.shape[0]
    s = x_prompt.shape[1]
    assert x_sample.shape[1] == s and s % ROW_TILE == 0 and s % KV_TILE == 0

    c_all = jnp.concatenate([c_prompt, c_sample], axis=0)
    pad = (-c_all.shape[0]) % V7X_SUBLANES
    c_pad = jnp.pad(c_all, ((0, pad), (0, 0)))
    mod = _modulation(c_pad, ada_w, ada_b)
    mod = mod.reshape(DEPTH, 3, c_pad.shape[0], 1, D_MODEL)

    w_in = attn_w_in[0]
    w_q = _deinterleave(w_in[:, :Q_DIM], N_HEADS)
    w_k = _deinterleave(w_in[:, Q_DIM:Q_DIM + KV_DIM], N_KV_HEADS)
    w_v = w_in[:, Q_DIM + KV_DIM:Q_DIM + 2 * KV_DIM]
    w_t = jnp.concatenate([w_q, w_v], axis=1).T.astype(BF16)
    w_kg = jnp.concatenate([w_k, w_in[:, Q_DIM + 2 * KV_DIM:]], axis=1).astype(BF16)
    cos_t, sin_t = _rope_tables(s)
    gq = _deinterleave(attn_q_norm[0], 1) * (HEAD_DIM ** -0.5 * LOG2_E)
    rope_q = tuple(t.T for t in _gained_rope_tables(cos_t, sin_t, gq))
    rope_k = _gained_rope_tables(cos_t, sin_t, _deinterleave(attn_k_norm[0], 1))
    small_logits = _logit_bound(attn_q_norm[0], attn_k_norm[0]) <= MAX_UNSHIFTED_LOGIT
    params = (norm_g, w_t, w_kg, rope_q, rope_k, small_logits, attn_w_out[0].astype(BF16),
              pool_w_in[0].astype(BF16), pool_w_group[0].astype(BF16),
              pool_scale[0].reshape(1, POOL_WIDTH), pool_w_out[0].astype(BF16))

    y_prompt = _trunk(x_prompt, mod[:, :, :bp], params)
    y_sample = _trunk(x_sample, mod[:, :, bp:bp + bs], params)
    return (y_prompt, y_sample)
```

```python
import functools

import jax
import jax.numpy as jnp
from jax import lax
from jax.experimental import pallas as pl
from jax.experimental.pallas import tpu as pltpu

D_MODEL = 1024
DEPTH = 2
N_HEADS = 8
N_KV_HEADS = 2
HEAD_DIM = 128
GROUP = N_HEADS // N_KV_HEADS
Q_DIM = N_HEADS * HEAD_DIM
KV_DIM = N_KV_HEADS * HEAD_DIM
ATTN_IN = 2 * Q_DIM + 2 * KV_DIM
AXIS_DIM = HEAD_DIM // 2
ROPE_THETA = 10000.0
GRID_W = 64
POOL_WINDOWS = (2, 4, 8, 16)
N_POOL_GROUPS = len(POOL_WINDOWS)
POOL_WIDTH = D_MODEL
POOL_GROUP_WIDTH = POOL_WIDTH // N_POOL_GROUPS
RMS_EPS = 1e-6
LOG2_E = 1.4426950408889634
MAX_UNSHIFTED_LOGIT = 40.0

V7X_SUBLANES = 8
V7X_LANES = 128
MXU_WIDTH = 256
V7X_VMEM_BYTES = 64 * 1024 * 1024
VMEM_LIMIT_BYTES = (V7X_VMEM_BYTES * 7) // 8

ROW_TILE = 1024
IN_PROJ_SLAB_ROWS = 256
Q_TILE = 2048
KV_TILE = 512
POOL_HALO = max(POOL_WINDOWS) // 2
HALO_BLOCK = 16

BF16 = jnp.bfloat16
F32 = jnp.float32
NT_DIMS = (((1,), (1,)), ((), ()))


def _compiler_params(n_grid_axes):
    return pltpu.CompilerParams(
        dimension_semantics=("arbitrary",) * n_grid_axes,
        vmem_limit_bytes=VMEM_LIMIT_BYTES,
    )


def _resident_spec(shape):
    return pl.BlockSpec(shape, lambda *_: (0,) * len(shape), pipeline_mode=pl.Buffered(1))


def _silu(x):
    return x * (1.0 / (1.0 + jnp.exp(-x)))


def _modulated_rmsnorm(x, norm_g, scl, shift):
    ms = jnp.mean(x * x, axis=-1, keepdims=True)
    return (x * lax.rsqrt(ms + RMS_EPS)) * norm_g * (1.0 + scl) + shift


def _mod_kernel(c_ref, w_ref, b_ref, o_ref):
    cs = _silu(c_ref[...])
    w = w_ref[0]
    cs_hi = cs.astype(BF16)
    cs_lo = (cs - cs_hi.astype(F32)).astype(BF16)
    w_hi = w.astype(BF16)
    w_lo = (w - w_hi.astype(F32)).astype(BF16)
    dot = functools.partial(jnp.dot, preferred_element_type=F32)
    o_ref[0, 0] = dot(cs_hi, w_hi) + (dot(cs_lo, w_hi) + dot(cs_hi, w_lo)) + b_ref[0, 0]


def _modulation(c_pad, ada_w, ada_b):
    bp = c_pad.shape[0]
    return pl.pallas_call(
        _mod_kernel,
        grid=(DEPTH, 3),
        in_specs=[
            pl.BlockSpec((bp, D_MODEL), lambda i, j: (0, 0)),
            pl.BlockSpec((1, D_MODEL, D_MODEL), lambda i, j: (i, 0, j)),
            pl.BlockSpec((1, 1, 1, D_MODEL), lambda i, j: (i, j, 0, 0)),
        ],
        out_specs=pl.BlockSpec((1, 1, bp, D_MODEL), lambda i, j: (i, j, 0, 0)),
        out_shape=jax.ShapeDtypeStruct((DEPTH, 3, bp, D_MODEL), F32),
        compiler_params=_compiler_params(2),
        name="adaln_modulation",
    )(c_pad, ada_w, ada_b.reshape(DEPTH, 3, 1, D_MODEL))


def _attn_in_kernel(x_ref, mod_ref, ng_ref, wt_ref, w_ref, qa_ref, qb_ref, ka_ref, kb_ref,
                    qt_ref, k_ref, vt_ref, sg_ref):
    tm = x_ref.shape[1]
    half = HEAD_DIM // 2
    slab = IN_PROJ_SLAB_ROWS
    hs, ys = [], []
    for r in range(tm // slab):
        rows = pl.ds(r * slab, slab)
        hs.append(_modulated_rmsnorm(x_ref[0, rows], ng_ref[...], mod_ref[1, 0], mod_ref[0, 0]).astype(BF16))
        ys.append(jnp.dot(hs[-1], w_ref[...], preferred_element_type=F32))
    h = jnp.concatenate(hs, axis=0)
    yt = lax.dot_general(wt_ref[...], h, NT_DIMS, preferred_element_type=F32)
    for r, y in enumerate(ys):
        rows = pl.ds(r * slab, slab)
        for j in range(N_KV_HEADS):
            t = y[:, j * HEAD_DIM:(j + 1) * HEAD_DIM]
            rs = lax.rsqrt(jnp.mean(t * t, axis=1, keepdims=True) + RMS_EPS)
            k_ref[0, j, rows] = ((t * ka_ref[rows] + pltpu.roll(t, half, 1) * kb_ref[rows]) * rs).astype(BF16)
        sg_ref[0, rows] = _silu(y[:, KV_DIM:]).astype(BF16)
    qa = qa_ref[...]
    qb = qb_ref[...]
    for hh in range(N_HEADS):
        t = yt[hh * HEAD_DIM:(hh + 1) * HEAD_DIM]
        rs = lax.rsqrt(jnp.mean(t * t, axis=0, keepdims=True) + RMS_EPS)
        partner = jnp.concatenate([t[half:], t[:half]], axis=0)
        qt_ref[0, hh] = ((t * qa + partner * qb) * rs).astype(BF16)
    for j in range(N_KV_HEADS):
        lo = Q_DIM + j * HEAD_DIM
        for c in range(tm // KV_TILE):
            vt_ref[0, j, c] = yt[lo:lo + HEAD_DIM, c * KV_TILE:(c + 1) * KV_TILE].astype(BF16)


def _attn_in_proj(x, mod, norm_g, w_t, w_kg, rope_q, rope_k):
    b, s, _ = x.shape
    tm = min(ROW_TILE, s)
    n_chunks = tm // KV_TILE
    return pl.pallas_call(
        _attn_in_kernel,
        grid=(b, s // tm),
        in_specs=[
            pl.BlockSpec((1, tm, D_MODEL), lambda bi, i: (bi, i, 0)),
            pl.BlockSpec((3, 1, 1, D_MODEL), lambda bi, i: (0, bi, 0, 0)),
            _resident_spec((1, D_MODEL)),
            _resident_spec((Q_DIM + KV_DIM, D_MODEL)),
            _resident_spec((D_MODEL, KV_DIM + Q_DIM)),
            pl.BlockSpec((HEAD_DIM, tm), lambda bi, i: (0, i)),
            pl.BlockSpec((HEAD_DIM, tm), lambda bi, i: (0, i)),
            pl.BlockSpec((tm, HEAD_DIM), lambda bi, i: (i, 0)),
            pl.BlockSpec((tm, HEAD_DIM), lambda bi, i: (i, 0)),
        ],
        out_specs=[
            pl.BlockSpec((1, N_HEADS, HEAD_DIM, tm), lambda bi, i: (bi, 0, 0, i)),
            pl.BlockSpec((1, N_KV_HEADS, tm, HEAD_DIM), lambda bi, i: (bi, 0, i, 0)),
            pl.BlockSpec((1, N_KV_HEADS, n_chunks, HEAD_DIM, KV_TILE), lambda bi, i: (bi, 0, i, 0, 0)),
            pl.BlockSpec((1, tm, Q_DIM), lambda bi, i: (bi, i, 0)),
        ],
        out_shape=[
            jax.ShapeDtypeStruct((b, N_HEADS, HEAD_DIM, s), BF16),
            jax.ShapeDtypeStruct((b, N_KV_HEADS, s, HEAD_DIM), BF16),
            jax.ShapeDtypeStruct((b, N_KV_HEADS, s // KV_TILE, HEAD_DIM, KV_TILE), BF16),
            jax.ShapeDtypeStruct((b, s, Q_DIM), BF16),
        ],
        compiler_params=_compiler_params(2),
        name="attn_in_proj",
    )(x, mod, norm_g, w_t, w_kg, *rope_q, *rope_k)


def _flash_kernel(qt_ref, k_ref, vt_ref, o_ref, *scratch, n_kv, tk, running_max):
    q_t = qt_ref[0, 0]
    tq = q_t.shape[1]

    def scores_t(j):
        return jnp.dot(k_ref[0, 0, pl.ds(j * tk, tk), :], q_t, preferred_element_type=F32)

    if running_max:
        m_sc, l_sc, acc_sc = scratch
        m_sc[...] = jnp.full(m_sc.shape, -jnp.inf, F32)
        l_sc[...] = jnp.zeros(l_sc.shape, F32)
        acc_sc[...] = jnp.zeros(acc_sc.shape, F32)

        def body(j, carry):
            s_t = scores_t(j)
            m_prev = m_sc[...]
            m_next = jnp.maximum(m_prev, jnp.max(s_t, axis=0, keepdims=True))
            alpha = jnp.exp2(m_prev - m_next)
            p_t = jnp.exp2(s_t - m_next)
            l_sc[...] = alpha * l_sc[...] + jnp.sum(p_t, axis=0, keepdims=True)
            acc_sc[...] = acc_sc[...] * alpha + jnp.dot(
                vt_ref[0, 0, j], p_t.astype(BF16), preferred_element_type=F32)
            m_sc[...] = m_next
            return carry

        lax.fori_loop(0, n_kv, body, 0)
        o_t = acc_sc[...] / l_sc[...]
    else:
        acc = jnp.zeros((HEAD_DIM, tq), F32)
        l_part = jnp.zeros((V7X_SUBLANES, tq), F32)
        s_next = scores_t(0)
        for j in range(n_kv):
            s_t = s_next
            if j + 1 < n_kv:
                s_next = scores_t(j + 1)
            p_t = jnp.exp2(s_t)
            l_part = l_part + jnp.sum(p_t.reshape(tk // V7X_SUBLANES, V7X_SUBLANES, tq), axis=0)
            acc = acc + jnp.dot(vt_ref[0, 0, j], p_t.astype(BF16), preferred_element_type=F32)
        o_t = acc / jnp.sum(l_part, axis=0, keepdims=True)
    o_ref[0, 0] = o_t.T.astype(BF16)


def _flash_attention(qt, k, vt, *, running_max):
    b, nh, dh, s = qt.shape
    nkv, n_kv, tk = vt.shape[1], vt.shape[2], vt.shape[4]
    group = nh // nkv
    tq = min(Q_TILE, s)
    assert s % tq == 0
    n_q = s // tq
    kern = functools.partial(_flash_kernel, n_kv=n_kv, tk=tk, running_max=running_max)
    stat = pltpu.VMEM((1, tq), F32)
    return pl.pallas_call(
        kern,
        grid=(b, nkv, group * n_q),
        in_specs=[
            pl.BlockSpec((1, 1, dh, tq), lambda bi, g, i: (bi, g * group + i // n_q, 0, i % n_q)),
            pl.BlockSpec((1, 1, s, dh), lambda bi, g, i: (bi, g, 0, 0)),
            pl.BlockSpec((1, 1, n_kv, dh, tk), lambda bi, g, i: (bi, g, 0, 0, 0)),
        ],
        out_specs=pl.BlockSpec((1, 1, tq, dh), lambda bi, g, i: (bi, g * group + i // n_q, i % n_q, 0)),
        out_shape=jax.ShapeDtypeStruct((b, nh, s, dh), BF16),
        scratch_shapes=[stat, stat, pltpu.VMEM((dh, tq), F32)] if running_max else [],
        compiler_params=_compiler_params(3),
        name="flash_attention_online" if running_max else "flash_attention",
    )(qt, k, vt)


def _out_pool_kernel(op_ref, o_ref, on_ref, sgp_ref, sg_ref, sgn_ref, xp_ref, x_ref, xn_ref,
                     mod0_ref, mod1_ref, wao_ref, ng_ref, win_ref, wg_ref, sc_ref, wout_ref, y_ref,
                     *, tm, seq):
    blk = HALO_BLOCK
    rows = tm + 2 * blk
    i = pl.program_id(1)
    n_tiles = pl.num_programs(1)

    def with_halo(prev, main, nxt):
        return jnp.concatenate([prev, main, nxt], axis=0)

    heads = lambda ref: jnp.concatenate([ref[0, hh] for hh in range(N_HEADS)], axis=1)
    o = with_halo(heads(op_ref), heads(o_ref), heads(on_ref))
    sg = with_halo(sgp_ref[0], sg_ref[0], sgn_ref[0])
    x0 = with_halo(xp_ref[0], x_ref[0], xn_ref[0])
    x1 = x0 + mod0_ref[2, 0] * jnp.dot(o * sg, wao_ref[...], preferred_element_type=F32)
    x = x1[blk:blk + tm]

    h = _modulated_rmsnorm(x1, ng_ref[...], mod1_ref[1, 0], mod1_ref[0, 0])
    y = jnp.dot(h.astype(BF16), win_ref[...], preferred_element_type=F32)
    u = y[:, :POOL_WIDTH]
    u = jnp.concatenate([jnp.where(i > 0, u[:blk], 0.0),
                         u[blk:blk + tm],
                         jnp.where(i < n_tiles - 1, u[blk + tm:], 0.0)], axis=0)
    gate = y[blk:blk + tm, POOL_WIDTH:]
    t = lax.broadcasted_iota(jnp.int32, (tm, 1), 0) + i * tm

    ahead = lambda a, d: pltpu.roll(a, rows - d, 0)
    behind = lambda a, d: pltpu.roll(a, d, 0)

    mixed = []
    for j, w in enumerate(POOL_WINDOWS):
        ug = u[:, j * POOL_GROUP_WIDTH:(j + 1) * POOL_GROUP_WIDTH]
        run, span = ug, 1
        while 2 * span < w:
            run, span = run + ahead(run, span), 2 * span
        win = behind(run, span) + run
        cnt = jnp.minimum(t + w // 2, seq) - jnp.maximum(t - w // 2, 0)
        pooled = win[blk:blk + tm] * (1.0 / cnt.astype(F32))
        mix = (pooled - ug[blk:blk + tm]).astype(BF16)
        mixed.append(jnp.dot(mix, wg_ref[j], preferred_element_type=F32))
    z = jnp.concatenate(mixed, axis=1) * sc_ref[...] * _silu(gate)
    out = jnp.dot(z.astype(BF16), wout_ref[...], preferred_element_type=F32)
    y_ref[0] = x + mod1_ref[2, 0] * out


def _out_proj_and_pool_layer(o, sg, x, mod0, mod1, w_attn_out, norm_g, w_in, w_group, scale, w_out):
    b, s, _ = x.shape
    tm = min(ROW_TILE, s)
    blk = HALO_BLOCK
    per = tm // blk
    last = s // blk - 1
    prev = lambda i: jnp.maximum(i * per - 1, 0)
    nxt = lambda i: jnp.minimum((i + 1) * per, last)
    mod_spec = pl.BlockSpec((3, 1, 1, D_MODEL), lambda bi, i: (0, bi, 0, 0))

    def rows_specs(width):
        return [pl.BlockSpec((1, blk, width), lambda bi, i: (bi, prev(i), 0)),
                pl.BlockSpec((1, tm, width), lambda bi, i: (bi, i, 0)),
                pl.BlockSpec((1, blk, width), lambda bi, i: (bi, nxt(i), 0))]

    kern = functools.partial(_out_pool_kernel, tm=tm, seq=s)
    return pl.pallas_call(
        kern,
        grid=(b, s // tm),
        in_specs=[
            pl.BlockSpec((1, N_HEADS, blk, HEAD_DIM), lambda bi, i: (bi, 0, prev(i), 0)),
            pl.BlockSpec((1, N_HEADS, tm, HEAD_DIM), lambda bi, i: (bi, 0, i, 0)),
            pl.BlockSpec((1, N_HEADS, blk, HEAD_DIM), lambda bi, i: (bi, 0, nxt(i), 0)),
            *rows_specs(Q_DIM),
            *rows_specs(D_MODEL),
            mod_spec,
            mod_spec,
            _resident_spec((Q_DIM, D_MODEL)),
            _resident_spec((1, D_MODEL)),
            _resident_spec((D_MODEL, 2 * POOL_WIDTH)),
            _resident_spec((N_POOL_GROUPS, POOL_GROUP_WIDTH, POOL_GROUP_WIDTH)),
            _resident_spec((1, POOL_WIDTH)),
            _resident_spec((POOL_WIDTH, D_MODEL)),
        ],
        out_specs=pl.BlockSpec((1, tm, D_MODEL), lambda bi, i: (bi, i, 0)),
        out_shape=jax.ShapeDtypeStruct((b, s, D_MODEL), F32),
        compiler_params=_compiler_params(2),
        name="out_proj_pool_layer",
    )(o, o, o, sg, sg, sg, x, x, x, mod0, mod1, w_attn_out, norm_g, w_in, w_group, scale, w_out)


def _rope_tables(s):
    rows = s // GRID_W
    inv = ROPE_THETA ** (-jnp.arange(0, AXIS_DIM, 2, dtype=F32) / AXIS_DIM)
    n_freq = inv.shape[0]
    row_ang = jnp.arange(rows, dtype=F32)[:, None] * inv
    col_ang = jnp.arange(GRID_W, dtype=F32)[:, None] * inv

    def on_grid(fn):
        r = jnp.broadcast_to(fn(row_ang)[:, None, :], (rows, GRID_W, n_freq))
        c = jnp.broadcast_to(fn(col_ang)[None, :, :], (rows, GRID_W, n_freq))
        return jnp.concatenate([r, c], axis=-1).reshape(s, 2 * n_freq)

    c, sn = on_grid(jnp.cos), on_grid(jnp.sin)
    return jnp.concatenate([c, c], axis=-1), jnp.concatenate([-sn, sn], axis=-1)


def _gained_rope_tables(cos_t, sin_t, gain):
    half = HEAD_DIM // 2
    partner_gain = jnp.concatenate([gain[half:], gain[:half]])
    return cos_t * gain, sin_t * partner_gain


def _logit_bound(q_gain, k_gain):
    return (HEAD_DIM ** 0.5 * 1.01) * jnp.max(jnp.abs(q_gain)) * jnp.max(jnp.abs(k_gain))


def _deinterleave(a, n_heads):
    lead = a.shape[:-1]
    a = a.reshape(lead + (n_heads, HEAD_DIM // 2, 2))
    a = jnp.concatenate([a[..., 0], a[..., 1]], axis=-1)
    return a.reshape(lead + (n_heads * HEAD_DIM,))


def _trunk(x, mod, params):
    (norm_g, w_t, w_kg, rope_q, rope_k, small_logits, attn_w_out,
     pool_w_in, pool_w_group, pool_scale, pool_w_out) = params
    qt, k, vt, sg = _attn_in_proj(x, mod[0], norm_g[0:1], w_t, w_kg, rope_q, rope_k)
    o = lax.cond(small_logits,
                 functools.partial(_flash_attention, running_max=False),
                 functools.partial(_flash_attention, running_max=True),
                 qt, k, vt)
    return _out_proj_and_pool_layer(o, sg, x, mod[0], mod[1], attn_w_out, norm_g[1:2],
                                    pool_w_in, pool_w_group, pool_scale, pool_w_out)


def kernel(x_prompt, x_sample, c_prompt, c_sample, norm_g, ada_w, ada_b, attn_w_in, attn_q_norm,
           attn_k_norm, attn_w_out, pool_w_in, pool_w_group, pool_scale, pool_w_out):
    assert DEPTH == 2 and attn_w_in.shape[0] == 1 and pool_w_in.shape[0] == 1
    bp, bs = x_prompt.shape[0], x_sample.shape[0]
    s = x_prompt.shape[1]
    assert x_sample.shape[1] == s and s % ROW_TILE == 0 and s % KV_TILE == 0

    c_all = jnp.concatenate([c_prompt, c_sample], axis=0)
    pad = (-c_all.shape[0]) % V7X_SUBLANES
    c_pad = jnp.pad(c_all, ((0, pad), (0, 0)))
    mod = _modulation(c_pad, ada_w, ada_b)
    mod = mod.reshape(DEPTH, 3, c_pad.shape[0], 1, D_MODEL)

    w_in = attn_w_in[0]
    w_q = _deinterleave(w_in[:, :Q_DIM], N_HEADS)
    w_k = _deinterleave(w_in[:, Q_DIM:Q_DIM + KV_DIM], N_KV_HEADS)
    w_v = w_in[:, Q_DIM + KV_DIM:Q_DIM + 2 * KV_DIM]
    w_t = jnp.concatenate([w_q, w_v], axis=1).T.astype(BF16)
    w_kg = jnp.concatenate([w_k, w_in[:, Q_DIM + 2 * KV_DIM:]], axis=1).astype(BF16)
    cos_t, sin_t = _rope_tables(s)
    gq = _deinterleave(attn_q_norm[0], 1) * (HEAD_DIM ** -0.5 * LOG2_E)
    rope_q = tuple(t.T for t in _gained_rope_tables(cos_t, sin_t, gq))
    rope_k = _gained_rope_tables(cos_t, sin_t, _deinterleave(attn_k_norm[0], 1))
    small_logits = _logit_bound(attn_q_norm[0], attn_k_norm[0]) <= MAX_UNSHIFTED_LOGIT
    params = (norm_g, w_t, w_kg, rope_q, rope_k, small_logits, attn_w_out[0].astype(BF16),
              pool_w_in[0].astype(BF16), pool_w_group[0].astype(BF16),
              pool_scale[0].reshape(1, POOL_WIDTH), pool_w_out[0].astype(BF16))

    y_prompt = _trunk(x_prompt, mod[:, :, :bp], params)
    y_sample = _trunk(x_sample, mod[:, :, bp:bp + bs], params)
    return (y_prompt, y_sample)
```

```python
import functools

import jax
import jax.numpy as jnp
from jax import lax
from jax.experimental import pallas as pl
from jax.experimental.pallas import tpu as pltpu

D_MODEL = 1024
DEPTH = 2
N_HEADS = 8
N_KV_HEADS = 2
HEAD_DIM = 128
GROUP = N_HEADS // N_KV_HEADS
Q_DIM = N_HEADS * HEAD_DIM
KV_DIM = N_KV_HEADS * HEAD_DIM
ATTN_IN = 2 * Q_DIM + 2 * KV_DIM
AXIS_DIM = HEAD_DIM // 2
ROPE_THETA = 10000.0
GRID_W = 64
POOL_WINDOWS = (2, 4, 8, 16)
N_POOL_GROUPS = len(POOL_WINDOWS)
POOL_WIDTH = D_MODEL
POOL_GROUP_WIDTH = POOL_WIDTH // N_POOL_GROUPS
RMS_EPS = 1e-6
LOG2_E = 1.4426950408889634
MAX_UNSHIFTED_LOGIT = 40.0

V7X_SUBLANES = 8
V7X_LANES = 128
MXU_WIDTH = 256
V7X_VMEM_BYTES = 64 * 1024 * 1024
VMEM_LIMIT_BYTES = (V7X_VMEM_BYTES * 7) // 8

ROW_TILE = 1024
IN_PROJ_SLAB_ROWS = 256
Q_TILE = 2048
KV_TILE = 512
POOL_HALO = max(POOL_WINDOWS) // 2
HALO_BLOCK = 16

BF16 = jnp.bfloat16
F32 = jnp.float32
NT_DIMS = (((1,), (1,)), ((), ()))


def _compiler_params(n_grid_axes):
    return pltpu.CompilerParams(
        dimension_semantics=("arbitrary",) * n_grid_axes,
        vmem_limit_bytes=VMEM_LIMIT_BYTES,
    )


def _resident_spec(shape):
    return pl.BlockSpec(shape, lambda *_: (0,) * len(shape), pipeline_mode=pl.Buffered(1))


def _silu(x):
    h = 0.5 * x
    return h + h * jnp.tanh(h)


def _modulated_rmsnorm(x, norm_g, scl, shift):
    gain = norm_g * (1.0 + scl)
    ms = jnp.mean(x * x, axis=-1, keepdims=True)
    return (x * lax.rsqrt(ms + RMS_EPS)) * gain + shift


def _mod_kernel(c_ref, w_ref, b_ref, o_ref):
    cs = _silu(c_ref[...])
    w = w_ref[0]
    cs_hi = cs.astype(BF16)
    cs_lo = (cs - cs_hi.astype(F32)).astype(BF16)
    w_hi = w.astype(BF16)
    w_lo = (w - w_hi.astype(F32)).astype(BF16)
    dot = functools.partial(jnp.dot, preferred_element_type=F32)
    o_ref[0, 0] = dot(cs_hi, w_hi) + (dot(cs_lo, w_hi) + dot(cs_hi, w_lo)) + b_ref[0, 0]


def _modulation(c_pad, ada_w, ada_b):
    bp = c_pad.shape[0]
    return pl.pallas_call(
        _mod_kernel,
        grid=(DEPTH, 3),
        in_specs=[
            pl.BlockSpec((bp, D_MODEL), lambda i, j: (0, 0)),
            pl.BlockSpec((1, D_MODEL, D_MODEL), lambda i, j: (i, 0, j)),
            pl.BlockSpec((1, 1, 1, D_MODEL), lambda i, j: (i, j, 0, 0)),
        ],
        out_specs=pl.BlockSpec((1, 1, bp, D_MODEL), lambda i, j: (i, j, 0, 0)),
        out_shape=jax.ShapeDtypeStruct((DEPTH, 3, bp, D_MODEL), F32),
        compiler_params=_compiler_params(2),
        name="adaln_modulation",
    )(c_pad, ada_w, ada_b.reshape(DEPTH, 3, 1, D_MODEL))


def _attn_in_kernel(x_ref, mod_ref, ng_ref, wt_ref, w_ref, qa_ref, qb_ref, ka_ref, kb_ref,
                    qt_ref, k_ref, vt_ref, sg_ref):
    tm = x_ref.shape[1]
    half = HEAD_DIM // 2
    slab = IN_PROJ_SLAB_ROWS
    hs, ys = [], []
    for r in range(tm // slab):
        rows = pl.ds(r * slab, slab)
        hs.append(_modulated_rmsnorm(x_ref[0, rows], ng_ref[...], mod_ref[1, 0], mod_ref[0, 0]).astype(BF16))
        ys.append(jnp.dot(hs[-1], w_ref[...], preferred_element_type=F32))
    h = jnp.concatenate(hs, axis=0)
    yt = lax.dot_general(wt_ref[...], h, NT_DIMS, preferred_element_type=F32)
    for r, y in enumerate(ys):
        rows = pl.ds(r * slab, slab)
        for j in range(N_KV_HEADS):
            t = y[:, j * HEAD_DIM:(j + 1) * HEAD_DIM]
            rs = lax.rsqrt(jnp.mean(t * t, axis=1, keepdims=True) + RMS_EPS)
            k_ref[0, j, rows] = ((t * ka_ref[rows] + pltpu.roll(t, half, 1) * kb_ref[rows]) * rs).astype(BF16)
        sg_ref[0, rows] = _silu(y[:, KV_DIM:]).astype(BF16)
    qa = qa_ref[...]
    qb = qb_ref[...]
    for hh in range(N_HEADS):
        t = yt[hh * HEAD_DIM:(hh + 1) * HEAD_DIM]
        rs = lax.rsqrt(jnp.mean(t * t, axis=0, keepdims=True) + RMS_EPS)
        partner = jnp.concatenate([t[half:], t[:half]], axis=0)
        qt_ref[0, hh] = ((t * qa + partner * qb) * rs).astype(BF16)
    for j in range(N_KV_HEADS):
        lo = Q_DIM + j * HEAD_DIM
        for c in range(tm // KV_TILE):
            vt_ref[0, j, c] = yt[lo:lo + HEAD_DIM, c * KV_TILE:(c + 1) * KV_TILE].astype(BF16)


def _attn_in_proj(x, mod, norm_g, w_t, w_kg, rope_q, rope_k):
    b, s, _ = x.shape
    tm = min(ROW_TILE, s)
    n_chunks = tm // KV_TILE
    return pl.pallas_call(
        _attn_in_kernel,
        grid=(b, s // tm),
        in_specs=[
            pl.BlockSpec((1, tm, D_MODEL), lambda bi, i: (bi, i, 0)),
            pl.BlockSpec((3, 1, 1, D_MODEL), lambda bi, i: (0, bi, 0, 0)),
            _resident_spec((1, D_MODEL)),
            _resident_spec((Q_DIM + KV_DIM, D_MODEL)),
            _resident_spec((D_MODEL, KV_DIM + Q_DIM)),
            pl.BlockSpec((HEAD_DIM, tm), lambda bi, i: (0, i)),
            pl.BlockSpec((HEAD_DIM, tm), lambda bi, i: (0, i)),
            pl.BlockSpec((tm, HEAD_DIM), lambda bi, i: (i, 0)),
            pl.BlockSpec((tm, HEAD_DIM), lambda bi, i: (i, 0)),
        ],
        out_specs=[
            pl.BlockSpec((1, N_HEADS, HEAD_DIM, tm), lambda bi, i: (bi, 0, 0, i)),
            pl.BlockSpec((1, N_KV_HEADS, tm, HEAD_DIM), lambda bi, i: (bi, 0, i, 0)),
            pl.BlockSpec((1, N_KV_HEADS, n_chunks, HEAD_DIM, KV_TILE), lambda bi, i: (bi, 0, i, 0, 0)),
            pl.BlockSpec((1, tm, Q_DIM), lambda bi, i: (bi, i, 0)),
        ],
        out_shape=[
            jax.ShapeDtypeStruct((b, N_HEADS, HEAD_DIM, s), BF16),
            jax.ShapeDtypeStruct((b, N_KV_HEADS, s, HEAD_DIM), BF16),
            jax.ShapeDtypeStruct((b, N_KV_HEADS, s // KV_TILE, HEAD_DIM, KV_TILE), BF16),
            jax.ShapeDtypeStruct((b, s, Q_DIM), BF16),
        ],
        compiler_params=_compiler_params(2),
        name="attn_in_proj",
    )(x, mod, norm_g, w_t, w_kg, *rope_q, *rope_k)


def _flash_kernel(qt_ref, k_ref, vt_ref, o_ref, *scratch, n_kv, tk, running_max):
    q_t = qt_ref[0, 0]
    tq = q_t.shape[1]

    def scores_t(j):
        return jnp.dot(k_ref[0, 0, pl.ds(j * tk, tk), :], q_t, preferred_element_type=F32)

    if running_max:
        m_sc, l_sc, acc_sc = scratch
        m_sc[...] = jnp.full(m_sc.shape, -jnp.inf, F32)
        l_sc[...] = jnp.zeros(l_sc.shape, F32)
        acc_sc[...] = jnp.zeros(acc_sc.shape, F32)

        def body(j, carry):
            s_t = scores_t(j)
            m_prev = m_sc[...]
            m_next = jnp.maximum(m_prev, jnp.max(s_t, axis=0, keepdims=True))
            alpha = jnp.exp2(m_prev - m_next)
            p_t = jnp.exp2(s_t - m_next)
            l_sc[...] = alpha * l_sc[...] + jnp.sum(p_t, axis=0, keepdims=True)
            acc_sc[...] = acc_sc[...] * alpha + jnp.dot(
                vt_ref[0, 0, j], p_t.astype(BF16), preferred_element_type=F32)
            m_sc[...] = m_next
            return carry

        lax.fori_loop(0, n_kv, body, 0)
        o_t = acc_sc[...] / l_sc[...]
    else:
        acc = jnp.zeros((HEAD_DIM, tq), F32)
        l_part = jnp.zeros((V7X_SUBLANES, tq), F32)
        s_next = scores_t(0)
        for j in range(n_kv):
            s_t = s_next
            if j + 1 < n_kv:
                s_next = scores_t(j + 1)
            p_t = jnp.exp2(s_t)
            l_part = l_part + jnp.sum(p_t.reshape(tk // V7X_SUBLANES, V7X_SUBLANES, tq), axis=0)
            acc = acc + jnp.dot(vt_ref[0, 0, j], p_t.astype(BF16), preferred_element_type=F32)
        o_t = acc / jnp.sum(l_part, axis=0, keepdims=True)
    o_ref[0, 0] = o_t.T.astype(BF16)


def _flash_attention(qt, k, vt, *, running_max):
    b, nh, dh, s = qt.shape
    nkv, n_kv, tk = vt.shape[1], vt.shape[2], vt.shape[4]
    group = nh // nkv
    tq = min(Q_TILE, s)
    assert s % tq == 0
    n_q = s // tq
    kern = functools.partial(_flash_kernel, n_kv=n_kv, tk=tk, running_max=running_max)
    stat = pltpu.VMEM((1, tq), F32)
    return pl.pallas_call(
        kern,
        grid=(b, nkv, group * n_q),
        in_specs=[
            pl.BlockSpec((1, 1, dh, tq), lambda bi, g, i: (bi, g * group + i // n_q, 0, i % n_q)),
            pl.BlockSpec((1, 1, s, dh), lambda bi, g, i: (bi, g, 0, 0)),
            pl.BlockSpec((1, 1, n_kv, dh, tk), lambda bi, g, i: (bi, g, 0, 0, 0)),
        ],
        out_specs=pl.BlockSpec((1, 1, tq, dh), lambda bi, g, i: (bi, g * group + i // n_q, i % n_q, 0)),
        out_shape=jax.ShapeDtypeStruct((b, nh, s, dh), BF16),
        scratch_shapes=[stat, stat, pltpu.VMEM((dh, tq), F32)] if running_max else [],
        compiler_params=_compiler_params(3),
        name="flash_attention_online" if running_max else "flash_attention",
    )(qt, k, vt)


def _out_pool_kernel(op_ref, o_ref, on_ref, sgp_ref, sg_ref, sgn_ref, xp_ref, x_ref, xn_ref,
                     mod0_ref, mod1_ref, wao_ref, ng_ref, win_ref, wg_ref, sc_ref, wout_ref, y_ref,
                     *, tm, seq):
    blk = HALO_BLOCK
    rows = tm + 2 * blk
    i = pl.program_id(1)
    n_tiles = pl.num_programs(1)

    def with_halo(prev, main, nxt):
        return jnp.concatenate([prev, main, nxt], axis=0)

    heads = lambda ref: jnp.concatenate([ref[0, hh] for hh in range(N_HEADS)], axis=1)
    o = with_halo(heads(op_ref), heads(o_ref), heads(on_ref))
    sg = with_halo(sgp_ref[0], sg_ref[0], sgn_ref[0])
    x0 = with_halo(xp_ref[0], x_ref[0], xn_ref[0])
    x1 = x0 + mod0_ref[2, 0] * jnp.dot(o * sg, wao_ref[...], preferred_element_type=F32)
    x = x1[blk:blk + tm]

    h = _modulated_rmsnorm(x1, ng_ref[...], mod1_ref[1, 0], mod1_ref[0, 0])
    y = jnp.dot(h.astype(BF16), win_ref[...], preferred_element_type=F32)
    u = y[:, :POOL_WIDTH]
    u = jnp.concatenate([jnp.where(i > 0, u[:blk], 0.0),
                         u[blk:blk + tm],
                         jnp.where(i < n_tiles - 1, u[blk + tm:], 0.0)], axis=0)
    gate = y[blk:blk + tm, POOL_WIDTH:]

    edge = V7X_SUBLANES
    t_first = lax.broadcasted_iota(jnp.int32, (edge, 1), 0) + i * tm
    t_last = t_first + (tm - edge)

    def clipped_window_fix(mean_w, w):
        def rescale(t):
            cnt = jnp.minimum(t + w // 2, seq) - jnp.maximum(t - w // 2, 0)
            return float(w) / cnt.astype(F32)
        return jnp.concatenate([mean_w[:edge] * rescale(t_first), mean_w[edge:tm - edge],
                                mean_w[tm - edge:] * rescale(t_last)], axis=0)

    ahead = lambda a, d: pltpu.roll(a, rows - d, 0)
    behind = lambda a, d: pltpu.roll(a, d, 0)

    def gated_out(j, mixed_j):
        cols = slice(j * POOL_GROUP_WIDTH, (j + 1) * POOL_GROUP_WIDTH)
        z = (mixed_j * sc_ref[:, cols] * _silu(gate[:, cols])).astype(BF16)
        return jnp.dot(z, wout_ref[cols, :], preferred_element_type=F32)

    out = None
    pending = None
    for j, w in enumerate(POOL_WINDOWS):
        ug = u[:, j * POOL_GROUP_WIDTH:(j + 1) * POOL_GROUP_WIDTH]
        run, span = ug, 1
        while 2 * span < w:
            run, span = run + ahead(run, span), 2 * span
        win = behind(run, span) + run
        pooled = clipped_window_fix(win[blk:blk + tm] * (1.0 / w), w)
        mix = (pooled - ug[blk:blk + tm]).astype(BF16)
        mixed_j = jnp.dot(mix, wg_ref[j], preferred_element_type=F32)
        if pending is not None:
            part = gated_out(*pending)
            out = part if out is None else out + part
        pending = (j, mixed_j)
    out = out + gated_out(*pending)
    y_ref[0] = x + mod1_ref[2, 0] * out


def _out_proj_and_pool_layer(o, sg, x, mod0, mod1, w_attn_out, norm_g, w_in, w_group, scale, w_out):
    b, s, _ = x.shape
    tm = min(ROW_TILE, s)
    blk = HALO_BLOCK
    per = tm // blk
    last = s // blk - 1
    prev = lambda i: jnp.maximum(i * per - 1, 0)
    nxt = lambda i: jnp.minimum((i + 1) * per, last)
    mod_spec = pl.BlockSpec((3, 1, 1, D_MODEL), lambda bi, i: (0, bi, 0, 0))

    def rows_specs(width):
        return [pl.BlockSpec((1, blk, width), lambda bi, i: (bi, prev(i), 0)),
                pl.BlockSpec((1, tm, width), lambda bi, i: (bi, i, 0)),
                pl.BlockSpec((1, blk, width), lambda bi, i: (bi, nxt(i), 0))]

    kern = functools.partial(_out_pool_kernel, tm=tm, seq=s)
    return pl.pallas_call(
        kern,
        grid=(b, s // tm),
        in_specs=[
            pl.BlockSpec((1, N_HEADS, blk, HEAD_DIM), lambda bi, i: (bi, 0, prev(i), 0)),
            pl.BlockSpec((1, N_HEADS, tm, HEAD_DIM), lambda bi, i: (bi, 0, i, 0)),
            pl.BlockSpec((1, N_HEADS, blk, HEAD_DIM), lambda bi, i: (bi, 0, nxt(i), 0)),
            *rows_specs(Q_DIM),
            *rows_specs(D_MODEL),
            mod_spec,
            mod_spec,
            _resident_spec((Q_DIM, D_MODEL)),
            _resident_spec((1, D_MODEL)),
            _resident_spec((D_MODEL, 2 * POOL_WIDTH)),
            _resident_spec((N_POOL_GROUPS, POOL_GROUP_WIDTH, POOL_GROUP_WIDTH)),
            _resident_spec((1, POOL_WIDTH)),
            _resident_spec((POOL_WIDTH, D_MODEL)),
        ],
        out_specs=pl.BlockSpec((1, tm, D_MODEL), lambda bi, i: (bi, i, 0)),
        out_shape=jax.ShapeDtypeStruct((b, s, D_MODEL), F32),
        compiler_params=_compiler_params(2),
        name="out_proj_pool_layer",
    )(o, o, o, sg, sg, sg, x, x, x, mod0, mod1, w_attn_out, norm_g, w_in, w_group, scale, w_out)


def _rope_tables(s):
    rows = s // GRID_W
    inv = ROPE_THETA ** (-jnp.arange(0, AXIS_DIM, 2, dtype=F32) / AXIS_DIM)
    n_freq = inv.shape[0]
    row_ang = jnp.arange(rows, dtype=F32)[:, None] * inv
    col_ang = jnp.arange(GRID_W, dtype=F32)[:, None] * inv

    def on_grid(fn):
        r = jnp.broadcast_to(fn(row_ang)[:, None, :], (rows, GRID_W, n_freq))
        c = jnp.broadcast_to(fn(col_ang)[None, :, :], (rows, GRID_W, n_freq))
        return jnp.concatenate([r, c], axis=-1).reshape(s, 2 * n_freq)

    c, sn = on_grid(jnp.cos), on_grid(jnp.sin)
    return jnp.concatenate([c, c], axis=-1), jnp.concatenate([-sn, sn], axis=-1)


def _gained_rope_tables(cos_t, sin_t, gain):
    half = HEAD_DIM // 2
    partner_gain = jnp.concatenate([gain[half:], gain[:half]])
    return cos_t * gain, sin_t * partner_gain


def _logit_bound(q_gain, k_gain):
    return (HEAD_DIM ** 0.5 * 1.01) * jnp.max(jnp.abs(q_gain)) * jnp.max(jnp.abs(k_gain))


def _deinterleave(a, n_heads):
    lead = a.shape[:-1]
    a = a.reshape(lead + (n_heads, HEAD_DIM // 2, 2))
    a = jnp.concatenate([a[..., 0], a[..., 1]], axis=-1)
    return a.reshape(lead + (n_heads * HEAD_DIM,))


def _trunk(x, mod, params):
    (norm_g, w_t, w_kg, rope_q, rope_k, small_logits, attn_w_out,
     pool_w_in, pool_w_group, pool_scale, pool_w_out) = params
    qt, k, vt, sg = _attn_in_proj(x, mod[0], norm_g[0:1], w_t, w_kg, rope_q, rope_k)
    o = lax.cond(small_logits,
                 functools.partial(_flash_attention, running_max=False),
                 functools.partial(_flash_attention, running_max=True),
                 qt, k, vt)
    return _out_proj_and_pool_layer(o, sg, x, mod[0], mod[1], attn_w_out, norm_g[1:2],
                                    pool_w_in, pool_w_group, pool_scale, pool_w_out)


def kernel(x_prompt, x_sample, c_prompt, c_sample, norm_g, ada_w, ada_b, attn_w_in, attn_q_norm,
           attn_k_norm, attn_w_out, pool_w_in, pool_w_group, pool_scale, pool_w_out):
    assert DEPTH == 2 and attn_w_in.shape[0] == 1 and pool_w_in.shape[0] == 1
    bp, bs = x_prompt.shape[0], x_sample.shape[0]
    s = x_prompt.shape[1]
    assert x_sample.shape[1] == s and s % ROW_TILE == 0 and s % KV_TILE == 0

    c_all = jnp.concatenate([c_prompt, c_sample], axis=0)
    pad = (-c_all.shape[0]) % V7X_SUBLANES
    c_pad = jnp.pad(c_all, ((0, pad), (0, 0)))
    mod = _modulation(c_pad, ada_w, ada_b)
    mod = mod.reshape(DEPTH, 3, c_pad.shape[0], 1, D_MODEL)

    w_in = attn_w_in[0]
    w_q = _deinterleave(w_in[:, :Q_DIM], N_HEADS)
    w_k = _deinterleave(w_in[:, Q_DIM:Q_DIM + KV_DIM], N_KV_HEADS)
    w_v = w_in[:, Q_DIM + KV_DIM:Q_DIM + 2 * KV_DIM]
    w_t = jnp.concatenate([w_q, w_v], axis=1).T.astype(BF16)
    w_kg = jnp.concatenate([w_k, w_in[:, Q_DIM + 2 * KV_DIM:]], axis=1).astype(BF16)
    cos_t, sin_t = _rope_tables(s)
    gq = _deinterleave(attn_q_norm[0], 1) * (HEAD_DIM ** -0.5 * LOG2_E)
    rope_q = tuple(t.T for t in _gained_rope_tables(cos_t, sin_t, gq))
    rope_k = _gained_rope_tables(cos_t, sin_t, _deinterleave(attn_k_norm[0], 1))
    small_logits = _logit_bound(attn_q_norm[0], attn_k_norm[0]) <= MAX_UNSHIFTED_LOGIT
    params = (norm_g, w_t, w_kg, rope_q, rope_k, small_logits, attn_w_out[0].astype(BF16),
              pool_w_in[0].astype(BF16), pool_w_group[0].astype(BF16),
              pool_scale[0].reshape(1, POOL_WIDTH), pool_w_out[0].astype(BF16))

    y_prompt = _trunk(x_prompt, mod[:, :, :bp], params)
    y_sample = _trunk(x_sample, mod[:, :, bp:bp + bs], params)
    return (y_prompt, y_sample)
```

```python
import functools

import jax
import jax.numpy as jnp
from jax import lax
from jax.experimental import pallas as pl
from jax.experimental.pallas import tpu as pltpu

D_MODEL = 1024
DEPTH = 2
N_HEADS = 8
N_KV_HEADS = 2
HEAD_DIM = 128
GROUP = N_HEADS // N_KV_HEADS
Q_DIM = N_HEADS * HEAD_DIM
KV_DIM = N_KV_HEADS * HEAD_DIM
ATTN_IN = 2 * Q_DIM + 2 * KV_DIM
AXIS_DIM = HEAD_DIM // 2
ROPE_THETA = 10000.0
GRID_W = 64
POOL_WINDOWS = (2, 4, 8, 16)
N_POOL_GROUPS = len(POOL_WINDOWS)
POOL_WIDTH = D_MODEL
POOL_GROUP_WIDTH = POOL_WIDTH // N_POOL_GROUPS
RMS_EPS = 1e-6
LOG2_E = 1.4426950408889634
MAX_UNSHIFTED_LOGIT = 40.0

V7X_SUBLANES = 8
V7X_LANES = 128
MXU_WIDTH = 256
V7X_VMEM_BYTES = 64 * 1024 * 1024
VMEM_LIMIT_BYTES = (V7X_VMEM_BYTES * 7) // 8

ROW_TILE = 1024
IN_PROJ_SLAB_ROWS = 256
Q_TILE = 4096
KV_TILE = 512
POOL_HALO = max(POOL_WINDOWS) // 2
HALO_BLOCK = 16

BF16 = jnp.bfloat16
F32 = jnp.float32
NT_DIMS = (((1,), (1,)), ((), ()))


def _compiler_params(n_grid_axes):
    return pltpu.CompilerParams(
        dimension_semantics=("arbitrary",) * n_grid_axes,
        vmem_limit_bytes=VMEM_LIMIT_BYTES,
    )


def _resident_spec(shape):
    return pl.BlockSpec(shape, lambda *_: (0,) * len(shape), pipeline_mode=pl.Buffered(1))


def _silu(x):
    h = 0.5 * x
    return h + h * jnp.tanh(h)


def _modulated_rmsnorm(x, norm_g, scl, shift):
    gain = norm_g * (1.0 + scl)
    ms = jnp.mean(x * x, axis=-1, keepdims=True)
    return (x * lax.rsqrt(ms + RMS_EPS)) * gain + shift


def _mod_kernel(c_ref, w_ref, b_ref, o_ref):
    cs = _silu(c_ref[...])
    w = w_ref[0]
    cs_hi = cs.astype(BF16)
    cs_lo = (cs - cs_hi.astype(F32)).astype(BF16)
    w_hi = w.astype(BF16)
    w_lo = (w - w_hi.astype(F32)).astype(BF16)
    dot = functools.partial(jnp.dot, preferred_element_type=F32)
    o_ref[0, 0] = dot(cs_hi, w_hi) + (dot(cs_lo, w_hi) + dot(cs_hi, w_lo)) + b_ref[0, 0]


def _modulation(c_pad, ada_w, ada_b):
    bp = c_pad.shape[0]
    return pl.pallas_call(
        _mod_kernel,
        grid=(DEPTH, 3),
        in_specs=[
            pl.BlockSpec((bp, D_MODEL), lambda i, j: (0, 0)),
            pl.BlockSpec((1, D_MODEL, D_MODEL), lambda i, j: (i, 0, j)),
            pl.BlockSpec((1, 1, 1, D_MODEL), lambda i, j: (i, j, 0, 0)),
        ],
        out_specs=pl.BlockSpec((1, 1, bp, D_MODEL), lambda i, j: (i, j, 0, 0)),
        out_shape=jax.ShapeDtypeStruct((DEPTH, 3, bp, D_MODEL), F32),
        compiler_params=_compiler_params(2),
        name="adaln_modulation",
    )(c_pad, ada_w, ada_b.reshape(DEPTH, 3, 1, D_MODEL))


def _attn_in_kernel(x_ref, mod_ref, ng_ref, wt_ref, w_ref, qa_ref, qb_ref, ka_ref, kb_ref,
                    qt_ref, k_ref, vt_ref, sg_ref):
    tm = x_ref.shape[1]
    half = HEAD_DIM // 2
    slab = IN_PROJ_SLAB_ROWS
    hs, ys = [], []
    for r in range(tm // slab):
        rows = pl.ds(r * slab, slab)
        hs.append(_modulated_rmsnorm(x_ref[0, rows], ng_ref[...], mod_ref[1, 0], mod_ref[0, 0]).astype(BF16))
        ys.append(jnp.dot(hs[-1], w_ref[...], preferred_element_type=F32))
    h = jnp.concatenate(hs, axis=0)
    yt = lax.dot_general(wt_ref[...], h, NT_DIMS, preferred_element_type=F32)
    for r, y in enumerate(ys):
        rows = pl.ds(r * slab, slab)
        for j in range(N_KV_HEADS):
            t = y[:, j * HEAD_DIM:(j + 1) * HEAD_DIM]
            rs = lax.rsqrt(jnp.mean(t * t, axis=1, keepdims=True) + RMS_EPS)
            k_ref[0, j, rows] = ((t * ka_ref[rows] + pltpu.roll(t, half, 1) * kb_ref[rows]) * rs).astype(BF16)
        sg_ref[0, rows] = _silu(y[:, KV_DIM:]).astype(BF16)
    qa = qa_ref[...]
    qb = qb_ref[...]
    for hh in range(N_HEADS):
        t = yt[hh * HEAD_DIM:(hh + 1) * HEAD_DIM]
        rs = lax.rsqrt(jnp.mean(t * t, axis=0, keepdims=True) + RMS_EPS)
        partner = jnp.concatenate([t[half:], t[:half]], axis=0)
        qt_ref[0, hh] = ((t * qa + partner * qb) * rs).astype(BF16)
    for j in range(N_KV_HEADS):
        lo = Q_DIM + j * HEAD_DIM
        for c in range(tm // KV_TILE):
            vt_ref[0, j, c] = yt[lo:lo + HEAD_DIM, c * KV_TILE:(c + 1) * KV_TILE].astype(BF16)


def _attn_in_proj(x, mod, norm_g, w_t, w_kg, rope_q, rope_k):
    b, s, _ = x.shape
    tm = min(ROW_TILE, s)
    n_chunks = tm // KV_TILE
    return pl.pallas_call(
        _attn_in_kernel,
        grid=(b, s // tm),
        in_specs=[
            pl.BlockSpec((1, tm, D_MODEL), lambda bi, i: (bi, i, 0)),
            pl.BlockSpec((3, 1, 1, D_MODEL), lambda bi, i: (0, bi, 0, 0)),
            _resident_spec((1, D_MODEL)),
            _resident_spec((Q_DIM + KV_DIM, D_MODEL)),
            _resident_spec((D_MODEL, KV_DIM + Q_DIM)),
            pl.BlockSpec((HEAD_DIM, tm), lambda bi, i: (0, i)),
            pl.BlockSpec((HEAD_DIM, tm), lambda bi, i: (0, i)),
            pl.BlockSpec((tm, HEAD_DIM), lambda bi, i: (i, 0)),
            pl.BlockSpec((tm, HEAD_DIM), lambda bi, i: (i, 0)),
        ],
        out_specs=[
            pl.BlockSpec((1, N_HEADS, HEAD_DIM, tm), lambda bi, i: (bi, 0, 0, i)),
            pl.BlockSpec((1, N_KV_HEADS, tm, HEAD_DIM), lambda bi, i: (bi, 0, i, 0)),
            pl.BlockSpec((1, N_KV_HEADS, n_chunks, HEAD_DIM, KV_TILE), lambda bi, i: (bi, 0, i, 0, 0)),
            pl.BlockSpec((1, tm, Q_DIM), lambda bi, i: (bi, i, 0)),
        ],
        out_shape=[
            jax.ShapeDtypeStruct((b, N_HEADS, HEAD_DIM, s), BF16),
            jax.ShapeDtypeStruct((b, N_KV_HEADS, s, HEAD_DIM), BF16),
            jax.ShapeDtypeStruct((b, N_KV_HEADS, s // KV_TILE, HEAD_DIM, KV_TILE), BF16),
            jax.ShapeDtypeStruct((b, s, Q_DIM), BF16),
        ],
        compiler_params=_compiler_params(2),
        name="attn_in_proj",
    )(x, mod, norm_g, w_t, w_kg, *rope_q, *rope_k)


def _flash_kernel(qt_ref, k_ref, vt_ref, o_ref, *scratch, n_kv, tk, running_max):
    q_t = qt_ref[0, 0]
    tq = q_t.shape[1]

    def scores_t(j):
        return jnp.dot(k_ref[0, 0, pl.ds(j * tk, tk), :], q_t, preferred_element_type=F32)

    if running_max:
        m_sc, l_sc, acc_sc = scratch
        m_sc[...] = jnp.full(m_sc.shape, -jnp.inf, F32)
        l_sc[...] = jnp.zeros(l_sc.shape, F32)
        acc_sc[...] = jnp.zeros(acc_sc.shape, F32)

        def body(j, carry):
            s_t = scores_t(j)
            m_prev = m_sc[...]
            m_next = jnp.maximum(m_prev, jnp.max(s_t, axis=0, keepdims=True))
            alpha = jnp.exp2(m_prev - m_next)
            p_t = jnp.exp2(s_t - m_next)
            l_sc[...] = alpha * l_sc[...] + jnp.sum(p_t, axis=0, keepdims=True)
            acc_sc[...] = acc_sc[...] * alpha + jnp.dot(
                vt_ref[0, 0, j], p_t.astype(BF16), preferred_element_type=F32)
            m_sc[...] = m_next
            return carry

        lax.fori_loop(0, n_kv, body, 0)
        o_t = acc_sc[...] / l_sc[...]
    else:
        acc = jnp.zeros((HEAD_DIM, tq), F32)
        l_part = jnp.zeros((V7X_SUBLANES, tq), F32)
        s_next = scores_t(0)
        for j in range(n_kv):
            s_t = s_next
            if j + 1 < n_kv:
                s_next = scores_t(j + 1)
            p_t = jnp.exp2(s_t)
            l_part = l_part + jnp.sum(p_t.reshape(tk // V7X_SUBLANES, V7X_SUBLANES, tq), axis=0)
            acc = acc + jnp.dot(vt_ref[0, 0, j], p_t.astype(BF16), preferred_element_type=F32)
        o_t = acc / jnp.sum(l_part, axis=0, keepdims=True)
    o_ref[0, 0] = o_t.T.astype(BF16)


def _flash_attention(qt, k, vt, *, running_max):
    b, nh, dh, s = qt.shape
    nkv, n_kv, tk = vt.shape[1], vt.shape[2], vt.shape[4]
    group = nh // nkv
    tq = min(Q_TILE, s)
    assert s % tq == 0
    n_q = s // tq
    kern = functools.partial(_flash_kernel, n_kv=n_kv, tk=tk, running_max=running_max)
    stat = pltpu.VMEM((1, tq), F32)
    return pl.pallas_call(
        kern,
        grid=(b, nkv, group * n_q),
        in_specs=[
            pl.BlockSpec((1, 1, dh, tq), lambda bi, g, i: (bi, g * group + i // n_q, 0, i % n_q)),
            pl.BlockSpec((1, 1, s, dh), lambda bi, g, i: (bi, g, 0, 0)),
            pl.BlockSpec((1, 1, n_kv, dh, tk), lambda bi, g, i: (bi, g, 0, 0, 0)),
        ],
        out_specs=pl.BlockSpec((1, 1, tq, dh), lambda bi, g, i: (bi, g * group + i // n_q, i % n_q, 0)),
        out_shape=jax.ShapeDtypeStruct((b, nh, s, dh), BF16),
        scratch_shapes=[stat, stat, pltpu.VMEM((dh, tq), F32)] if running_max else [],
        compiler_params=_compiler_params(3),
        name="flash_attention_online" if running_max else "flash_attention",
    )(qt, k, vt)


def _out_pool_kernel(op_ref, o_ref, on_ref, sgp_ref, sg_ref, sgn_ref, xp_ref, x_ref, xn_ref,
                     mod0_ref, mod1_ref, wao_ref, ng_ref, win_ref, wg_ref, sc_ref, wout_ref, y_ref,
                     *, tm, seq):
    blk = HALO_BLOCK
    rows = tm + 2 * blk
    i = pl.program_id(1)
    n_tiles = pl.num_programs(1)

    def with_halo(prev, main, nxt):
        return jnp.concatenate([prev, main, nxt], axis=0)

    heads = lambda ref: jnp.concatenate([ref[0, hh] for hh in range(N_HEADS)], axis=1)
    o = with_halo(heads(op_ref), heads(o_ref), heads(on_ref))
    sg = with_halo(sgp_ref[0], sg_ref[0], sgn_ref[0])
    x0 = with_halo(xp_ref[0], x_ref[0], xn_ref[0])
    x1 = x0 + mod0_ref[2, 0] * jnp.dot(o * sg, wao_ref[...], preferred_element_type=F32)
    x = x1[blk:blk + tm]

    h = _modulated_rmsnorm(x1, ng_ref[...], mod1_ref[1, 0], mod1_ref[0, 0])
    y = jnp.dot(h.astype(BF16), win_ref[...], preferred_element_type=F32)
    u = y[:, :POOL_WIDTH]
    u = jnp.concatenate([jnp.where(i > 0, u[:blk], 0.0),
                         u[blk:blk + tm],
                         jnp.where(i < n_tiles - 1, u[blk + tm:], 0.0)], axis=0)
    gate = y[blk:blk + tm, POOL_WIDTH:]

    edge = V7X_SUBLANES
    t_first = lax.broadcasted_iota(jnp.int32, (edge, 1), 0) + i * tm
    t_last = t_first + (tm - edge)

    def clipped_window_fix(mean_w, w):
        def rescale(t):
            cnt = jnp.minimum(t + w // 2, seq) - jnp.maximum(t - w // 2, 0)
            return float(w) / cnt.astype(F32)
        return jnp.concatenate([mean_w[:edge] * rescale(t_first), mean_w[edge:tm - edge],
                                mean_w[tm - edge:] * rescale(t_last)], axis=0)

    ahead = lambda a, d: pltpu.roll(a, rows - d, 0)
    behind = lambda a, d: pltpu.roll(a, d, 0)

    def gated_out(j, mixed_j):
        cols = slice(j * POOL_GROUP_WIDTH, (j + 1) * POOL_GROUP_WIDTH)
        z = (mixed_j * sc_ref[:, cols] * _silu(gate[:, cols])).astype(BF16)
        return jnp.dot(z, wout_ref[cols, :], preferred_element_type=F32)

    out = None
    pending = None
    for j, w in enumerate(POOL_WINDOWS):
        ug = u[:, j * POOL_GROUP_WIDTH:(j + 1) * POOL_GROUP_WIDTH]
        run, span = ug, 1
        while 2 * span < w:
            run, span = run + ahead(run, span), 2 * span
        win = behind(run, span) + run
        pooled = clipped_window_fix(win[blk:blk + tm] * (1.0 / w), w)
        mix = (pooled - ug[blk:blk + tm]).astype(BF16)
        mixed_j = jnp.dot(mix, wg_ref[j], preferred_element_type=F32)
        if pending is not None:
            part = gated_out(*pending)
            out = part if out is None else out + part
        pending = (j, mixed_j)
    out = out + gated_out(*pending)
    y_ref[0] = x + mod1_ref[2, 0] * out


def _out_proj_and_pool_layer(o, sg, x, mod0, mod1, w_attn_out, norm_g, w_in, w_group, scale, w_out):
    b, s, _ = x.shape
    tm = min(ROW_TILE, s)
    blk = HALO_BLOCK
    per = tm // blk
    last = s // blk - 1
    prev = lambda i: jnp.maximum(i * per - 1, 0)
    nxt = lambda i: jnp.minimum((i + 1) * per, last)
    mod_spec = pl.BlockSpec((3, 1, 1, D_MODEL), lambda bi, i: (0, bi, 0, 0))

    def rows_specs(width):
        return [pl.BlockSpec((1, blk, width), lambda bi, i: (bi, prev(i), 0)),
                pl.BlockSpec((1, tm, width), lambda bi, i: (bi, i, 0)),
                pl.BlockSpec((1, blk, width), lambda bi, i: (bi, nxt(i), 0))]

    kern = functools.partial(_out_pool_kernel, tm=tm, seq=s)
    return pl.pallas_call(
        kern,
        grid=(b, s // tm),
        in_specs=[
            pl.BlockSpec((1, N_HEADS, blk, HEAD_DIM), lambda bi, i: (bi, 0, prev(i), 0)),
            pl.BlockSpec((1, N_HEADS, tm, HEAD_DIM), lambda bi, i: (bi, 0, i, 0)),
            pl.BlockSpec((1, N_HEADS, blk, HEAD_DIM), lambda bi, i: (bi, 0, nxt(i), 0)),
            *rows_specs(Q_DIM),
            *rows_specs(D_MODEL),
            mod_spec,
            mod_spec,
            _resident_spec((Q_DIM, D_MODEL)),
            _resident_spec((1, D_MODEL)),
            _resident_spec((D_MODEL, 2 * POOL_WIDTH)),
            _resident_spec((N_POOL_GROUPS, POOL_GROUP_WIDTH, POOL_GROUP_WIDTH)),
            _resident_spec((1, POOL_WIDTH)),
            _resident_spec((POOL_WIDTH, D_MODEL)),
        ],
        out_specs=pl.BlockSpec((1, tm, D_MODEL), lambda bi, i: (bi, i, 0)),
        out_shape=jax.ShapeDtypeStruct((b, s, D_MODEL), F32),
        compiler_params=_compiler_params(2),
        name="out_proj_pool_layer",
    )(o, o, o, sg, sg, sg, x, x, x, mod0, mod1, w_attn_out, norm_g, w_in, w_group, scale, w_out)


def _rope_tables(s):
    rows = s // GRID_W
    inv = ROPE_THETA ** (-jnp.arange(0, AXIS_DIM, 2, dtype=F32) / AXIS_DIM)
    n_freq = inv.shape[0]
    row_ang = jnp.arange(rows, dtype=F32)[:, None] * inv
    col_ang = jnp.arange(GRID_W, dtype=F32)[:, None] * inv

    def on_grid(fn):
        r = jnp.broadcast_to(fn(row_ang)[:, None, :], (rows, GRID_W, n_freq))
        c = jnp.broadcast_to(fn(col_ang)[None, :, :], (rows, GRID_W, n_freq))
        return jnp.concatenate([r, c], axis=-1).reshape(s, 2 * n_freq)

    c, sn = on_grid(jnp.cos), on_grid(jnp.sin)
    return jnp.concatenate([c, c], axis=-1), jnp.concatenate([-sn, sn], axis=-1)


def _gained_rope_tables(cos_t, sin_t, gain):
    half = HEAD_DIM // 2
    partner_gain = jnp.concatenate([gain[half:], gain[:half]])
    return cos_t * gain, sin_t * partner_gain


def _logit_bound(q_gain, k_gain):
    return (HEAD_DIM ** 0.5 * 1.01) * jnp.max(jnp.abs(q_gain)) * jnp.max(jnp.abs(k_gain))


def _deinterleave(a, n_heads):
    lead = a.shape[:-1]
    a = a.reshape(lead + (n_heads, HEAD_DIM // 2, 2))
    a = jnp.concatenate([a[..., 0], a[..., 1]], axis=-1)
    return a.reshape(lead + (n_heads * HEAD_DIM,))


def _trunk(x, mod, params):
    (norm_g, w_t, w_kg, rope_q, rope_k, small_logits, attn_w_out,
     pool_w_in, pool_w_group, pool_scale, pool_w_out) = params
    qt, k, vt, sg = _attn_in_proj(x, mod[0], norm_g[0:1], w_t, w_kg, rope_q, rope_k)
    o = lax.cond(small_logits,
                 functools.partial(_flash_attention, running_max=False),
                 functools.partial(_flash_attention, running_max=True),
                 qt, k, vt)
    return _out_proj_and_pool_layer(o, sg, x, mod[0], mod[1], attn_w_out, norm_g[1:2],
                                    pool_w_in, pool_w_group, pool_scale, pool_w_out)


def kernel(x_prompt, x_sample, c_prompt, c_sample, norm_g, ada_w, ada_b, attn_w_in, attn_q_norm,
           attn_k_norm, attn_w_out, pool_w_in, pool_w_group, pool_scale, pool_w_out):
    assert DEPTH == 2 and attn_w_in.shape[0] == 1 and pool_w_in.shape[0] == 1
    bp, bs = x_prompt.shape[0], x_sample.shape[0]
    s = x_prompt.shape[1]
    assert x_sample.shape[1] == s and s % ROW_TILE == 0 and s % KV_TILE == 0

    c_all = jnp.concatenate([c_prompt, c_sample], axis=0)
    pad = (-c_all.shape[0]) % V7X_SUBLANES
    c_pad = jnp.pad(c_all, ((0, pad), (0, 0)))
    mod = _modulation(c_pad, ada_w, ada_b)
    mod = mod.reshape(DEPTH, 3, c_pad.shape[0], 1, D_MODEL)

    w_in = attn_w_in[0]
    w_q = _deinterleave(w_in[:, :Q_DIM], N_HEADS)
    w_k = _deinterleave(w_in[:, Q_DIM:Q_DIM + KV_DIM], N_KV_HEADS)
    w_v = w_in[:, Q_DIM + KV_DIM:Q_DIM + 2 * KV_DIM]
    w_t = jnp.concatenate([w_q, w_v], axis=1).T.astype(BF16)
    w_kg = jnp.concatenate([w_k, w_in[:, Q_DIM + 2 * KV_DIM:]], axis=1).astype(BF16)
    cos_t, sin_t = _rope_tables(s)
    gq = _deinterleave(attn_q_norm[0], 1) * (HEAD_DIM ** -0.5 * LOG2_E)
    rope_q = tuple(t.T for t in _gained_rope_tables(cos_t, sin_t, gq))
    rope_k = _gained_rope_tables(cos_t, sin_t, _deinterleave(attn_k_norm[0], 1))
    small_logits = _logit_bound(attn_q_norm[0], attn_k_norm[0]) <= MAX_UNSHIFTED_LOGIT
    params = (norm_g, w_t, w_kg, rope_q, rope_k, small_logits, attn_w_out[0].astype(BF16),
              pool_w_in[0].astype(BF16), pool_w_group[0].astype(BF16),
              pool_scale[0].reshape(1, POOL_WIDTH), pool_w_out[0].astype(BF16))

    y_prompt = _trunk(x_prompt, mod[:, :, :bp], params)
    y_sample = _trunk(x_sample, mod[:, :, bp:bp + bs], params)
    return (y_prompt, y_sample)
```

```python
import functools

import jax
import jax.numpy as jnp
from jax import lax
from jax.experimental import pallas as pl
from jax.experimental.pallas import tpu as pltpu

D_MODEL = 1024
DEPTH = 2
N_HEADS = 8
N_KV_HEADS = 2
HEAD_DIM = 128
Q_DIM = N_HEADS * HEAD_DIM
KV_DIM = N_KV_HEADS * HEAD_DIM
AXIS_DIM = HEAD_DIM // 2
ROPE_THETA = 10000.0
GRID_W = 64
POOL_WINDOWS = (2, 4, 8, 16)
N_POOL_GROUPS = len(POOL_WINDOWS)
POOL_WIDTH = D_MODEL
POOL_GROUP_WIDTH = POOL_WIDTH // N_POOL_GROUPS
RMS_EPS = 1e-6
LOG2_E = 1.4426950408889634
MAX_UNSHIFTED_LOGIT = 40.0

V7X_SUBLANES = 8
V7X_LANES = 128
V7X_VMEM_BYTES = 64 * 1024 * 1024
VMEM_LIMIT_BYTES = (V7X_VMEM_BYTES * 7) // 8

ROW_TILE = 1024
IN_PROJ_SLAB_ROWS = 256
Q_TILE = 4096
KV_TILE = 512
POOL_HALO = max(POOL_WINDOWS) // 2
HALO_BLOCK = 16
assert POOL_HALO <= V7X_SUBLANES and 2 * POOL_HALO <= HALO_BLOCK

BF16 = jnp.bfloat16
F32 = jnp.float32
NT_DIMS = (((1,), (1,)), ((), ()))


def _compiler_params(n_grid_axes):
    return pltpu.CompilerParams(
        dimension_semantics=("arbitrary",) * n_grid_axes,
        vmem_limit_bytes=VMEM_LIMIT_BYTES,
    )


def _resident_spec(shape):
    return pl.BlockSpec(shape, lambda *_: (0,) * len(shape), pipeline_mode=pl.Buffered(1))


def _silu(x):
    h = 0.5 * x
    return h + h * jnp.tanh(h)


def _modulated_rmsnorm(x, norm_g, scl, shift):
    gain = norm_g * (1.0 + scl)
    ms = jnp.mean(x * x, axis=-1, keepdims=True)
    return (x * lax.rsqrt(ms + RMS_EPS)) * gain + shift


def _mod_kernel(c_ref, w_ref, b_ref, o_ref):
    cs = _silu(c_ref[...])
    w = w_ref[0]
    cs_hi = cs.astype(BF16)
    cs_lo = (cs - cs_hi.astype(F32)).astype(BF16)
    w_hi = w.astype(BF16)
    w_lo = (w - w_hi.astype(F32)).astype(BF16)
    dot = functools.partial(jnp.dot, preferred_element_type=F32)
    o_ref[0, 0] = dot(cs_hi, w_hi) + (dot(cs_lo, w_hi) + dot(cs_hi, w_lo)) + b_ref[0, 0]


def _modulation(c_pad, ada_w, ada_b):
    bp = c_pad.shape[0]
    return pl.pallas_call(
        _mod_kernel,
        grid=(DEPTH, 3),
        in_specs=[
            pl.BlockSpec((bp, D_MODEL), lambda i, j: (0, 0)),
            pl.BlockSpec((1, D_MODEL, D_MODEL), lambda i, j: (i, 0, j)),
            pl.BlockSpec((1, 1, 1, D_MODEL), lambda i, j: (i, j, 0, 0)),
        ],
        out_specs=pl.BlockSpec((1, 1, bp, D_MODEL), lambda i, j: (i, j, 0, 0)),
        out_shape=jax.ShapeDtypeStruct((DEPTH, 3, bp, D_MODEL), F32),
        compiler_params=_compiler_params(2),
        name="adaln_modulation",
    )(c_pad, ada_w, ada_b.reshape(DEPTH, 3, 1, D_MODEL))


def _attn_in_kernel(x_ref, mod_ref, ng_ref, wt_ref, w_ref, qa_ref, qb_ref, ka_ref, kb_ref,
                    qt_ref, k_ref, vt_ref, sg_ref):
    tm = x_ref.shape[1]
    half = HEAD_DIM // 2
    slab = IN_PROJ_SLAB_ROWS
    hs, ys = [], []
    for r in range(tm // slab):
        rows = pl.ds(r * slab, slab)
        hs.append(_modulated_rmsnorm(x_ref[0, rows], ng_ref[...], mod_ref[1, 0], mod_ref[0, 0]).astype(BF16))
        ys.append(jnp.dot(hs[-1], w_ref[...], preferred_element_type=F32))
    h = jnp.concatenate(hs, axis=0)
    yt = lax.dot_general(wt_ref[...], h, NT_DIMS, preferred_element_type=F32)
    for r, y in enumerate(ys):
        rows = pl.ds(r * slab, slab)
        for j in range(N_KV_HEADS):
            t = y[:, j * HEAD_DIM:(j + 1) * HEAD_DIM]
            rs = lax.rsqrt(jnp.mean(t * t, axis=1, keepdims=True) + RMS_EPS)
            k_ref[0, j, rows] = ((t * ka_ref[rows] + pltpu.roll(t, half, 1) * kb_ref[rows]) * rs).astype(BF16)
        sg_ref[0, rows] = _silu(y[:, KV_DIM:]).astype(BF16)
    qa = qa_ref[...]
    qb = qb_ref[...]
    for hh in range(N_HEADS):
        t = yt[hh * HEAD_DIM:(hh + 1) * HEAD_DIM]
        rs = lax.rsqrt(jnp.mean(t * t, axis=0, keepdims=True) + RMS_EPS)
        partner = jnp.concatenate([t[half:], t[:half]], axis=0)
        qt_ref[0, hh] = ((t * qa + partner * qb) * rs).astype(BF16)
    for j in range(N_KV_HEADS):
        lo = Q_DIM + j * HEAD_DIM
        for c in range(tm // KV_TILE):
            vt_ref[0, j, c] = yt[lo:lo + HEAD_DIM, c * KV_TILE:(c + 1) * KV_TILE].astype(BF16)


def _attn_in_proj(x, mod, norm_g, w_t, w_kg, rope_q, rope_k):
    b, s, _ = x.shape
    tm = min(ROW_TILE, s)
    n_chunks = tm // KV_TILE
    return pl.pallas_call(
        _attn_in_kernel,
        grid=(b, s // tm),
        in_specs=[
            pl.BlockSpec((1, tm, D_MODEL), lambda bi, i: (bi, i, 0)),
            pl.BlockSpec((3, 1, 1, D_MODEL), lambda bi, i: (0, bi, 0, 0)),
            _resident_spec((1, D_MODEL)),
            _resident_spec((Q_DIM + KV_DIM, D_MODEL)),
            _resident_spec((D_MODEL, KV_DIM + Q_DIM)),
            pl.BlockSpec((HEAD_DIM, tm), lambda bi, i: (0, i)),
            pl.BlockSpec((HEAD_DIM, tm), lambda bi, i: (0, i)),
            pl.BlockSpec((tm, HEAD_DIM), lambda bi, i: (i, 0)),
            pl.BlockSpec((tm, HEAD_DIM), lambda bi, i: (i, 0)),
        ],
        out_specs=[
            pl.BlockSpec((1, N_HEADS, HEAD_DIM, tm), lambda bi, i: (bi, 0, 0, i)),
            pl.BlockSpec((1, N_KV_HEADS, tm, HEAD_DIM), lambda bi, i: (bi, 0, i, 0)),
            pl.BlockSpec((1, N_KV_HEADS, n_chunks, HEAD_DIM, KV_TILE), lambda bi, i: (bi, 0, i, 0, 0)),
            pl.BlockSpec((1, tm, Q_DIM), lambda bi, i: (bi, i, 0)),
        ],
        out_shape=[
            jax.ShapeDtypeStruct((b, N_HEADS, HEAD_DIM, s), BF16),
            jax.ShapeDtypeStruct((b, N_KV_HEADS, s, HEAD_DIM), BF16),
            jax.ShapeDtypeStruct((b, N_KV_HEADS, s // KV_TILE, HEAD_DIM, KV_TILE), BF16),
            jax.ShapeDtypeStruct((b, s, Q_DIM), BF16),
        ],
        compiler_params=_compiler_params(2),
        name="attn_in_proj",
    )(x, mod, norm_g, w_t, w_kg, *rope_q, *rope_k)


def _flash_kernel(qt_ref, k_ref, vt_ref, o_ref, *scratch, n_kv, tk, running_max):
    q_t = qt_ref[0, 0]
    tq = q_t.shape[1]

    def scores_t(j):
        return jnp.dot(k_ref[0, 0, pl.ds(j * tk, tk), :], q_t, preferred_element_type=F32)

    if running_max:
        m_sc, l_sc, acc_sc = scratch
        m_sc[...] = jnp.full(m_sc.shape, -jnp.inf, F32)
        l_sc[...] = jnp.zeros(l_sc.shape, F32)
        acc_sc[...] = jnp.zeros(acc_sc.shape, F32)

        def body(j, carry):
            s_t = scores_t(j)
            m_prev = m_sc[...]
            m_next = jnp.maximum(m_prev, jnp.max(s_t, axis=0, keepdims=True))
            alpha = jnp.exp2(m_prev - m_next)
            p_t = jnp.exp2(s_t - m_next)
            l_sc[...] = alpha * l_sc[...] + jnp.sum(p_t, axis=0, keepdims=True)
            acc_sc[...] = acc_sc[...] * alpha + jnp.dot(
                vt_ref[0, 0, j], p_t.astype(BF16), preferred_element_type=F32)
            m_sc[...] = m_next
            return carry

        lax.fori_loop(0, n_kv, body, 0)
        o_t = acc_sc[...] / l_sc[...]
    else:
        acc = jnp.zeros((HEAD_DIM, tq), F32)
        l_part = jnp.zeros((V7X_SUBLANES, tq), F32)
        s_next = scores_t(0)
        for j in range(n_kv):
            s_t = s_next
            if j + 1 < n_kv:
                s_next = scores_t(j + 1)
            p_t = jnp.exp2(s_t)
            l_part = l_part + jnp.sum(p_t.reshape(tk // V7X_SUBLANES, V7X_SUBLANES, tq), axis=0)
            acc = acc + jnp.dot(vt_ref[0, 0, j], p_t.astype(BF16), preferred_element_type=F32)
        o_t = acc / jnp.sum(l_part, axis=0, keepdims=True)
    o_ref[0, 0] = o_t.T.astype(BF16)


def _flash_attention(qt, k, vt, *, running_max):
    b, nh, dh, s = qt.shape
    nkv, n_kv, tk = vt.shape[1], vt.shape[2], vt.shape[4]
    group = nh // nkv
    tq = min(Q_TILE, s)
    assert s % tq == 0
    n_q = s // tq
    kern = functools.partial(_flash_kernel, n_kv=n_kv, tk=tk, running_max=running_max)
    stat = pltpu.VMEM((1, tq), F32)
    return pl.pallas_call(
        kern,
        grid=(b, nkv, group * n_q),
        in_specs=[
            pl.BlockSpec((1, 1, dh, tq), lambda bi, g, i: (bi, g * group + i // n_q, 0, i % n_q)),
            pl.BlockSpec((1, 1, s, dh), lambda bi, g, i: (bi, g, 0, 0)),
            pl.BlockSpec((1, 1, n_kv, dh, tk), lambda bi, g, i: (bi, g, 0, 0, 0)),
        ],
        out_specs=pl.BlockSpec((1, 1, tq, dh), lambda bi, g, i: (bi, g * group + i // n_q, i % n_q, 0)),
        out_shape=jax.ShapeDtypeStruct((b, nh, s, dh), BF16),
        scratch_shapes=[stat, stat, pltpu.VMEM((dh, tq), F32)] if running_max else [],
        compiler_params=_compiler_params(3),
        name="flash_attention_online" if running_max else "flash_attention",
    )(qt, k, vt)


def _out_pool_kernel(op_ref, o_ref, on_ref, sgp_ref, sg_ref, sgn_ref, xp_ref, x_ref, xn_ref,
                     mod0_ref, mod1_ref, wao_ref, ng_ref, win_ref, wg_ref, sc_ref, wout_ref, y_ref,
                     *, tm, seq):
    blk = HALO_BLOCK
    rows = tm + 2 * blk
    i = pl.program_id(1)
    n_tiles = pl.num_programs(1)

    def with_halo(prev, main, nxt):
        return jnp.concatenate([prev, main, nxt], axis=0)

    heads = lambda ref: jnp.concatenate([ref[0, hh] for hh in range(N_HEADS)], axis=1)
    o = with_halo(heads(op_ref), heads(o_ref), heads(on_ref))
    sg = with_halo(sgp_ref[0], sg_ref[0], sgn_ref[0])
    x0 = with_halo(xp_ref[0], x_ref[0], xn_ref[0])
    gated = o * sg
    halves = ((0, blk + tm // 2), (blk + tm // 2, rows))
    x1_parts = [x0[lo:hi] + mod0_ref[2, 0] * jnp.dot(gated[lo:hi], wao_ref[...], preferred_element_type=F32)
                for lo, hi in halves]
    y_parts = [jnp.dot(_modulated_rmsnorm(part, ng_ref[...], mod1_ref[1, 0], mod1_ref[0, 0]).astype(BF16),
                       win_ref[...], preferred_element_type=F32) for part in x1_parts]
    x = jnp.concatenate(x1_parts, axis=0)[blk:blk + tm]
    y = jnp.concatenate(y_parts, axis=0)
    u = y[:, :POOL_WIDTH]
    u = jnp.concatenate([jnp.where(i > 0, u[:blk], 0.0),
                         u[blk:blk + tm],
                         jnp.where(i < n_tiles - 1, u[blk + tm:], 0.0)], axis=0)
    gate = y[blk:blk + tm, POOL_WIDTH:]

    edge = V7X_SUBLANES
    t_first = lax.broadcasted_iota(jnp.int32, (edge, 1), 0) + i * tm
    t_last = t_first + (tm - edge)

    def clipped_window_fix(mean_w, w):
        def rescale(t):
            cnt = jnp.minimum(t + w // 2, seq) - jnp.maximum(t - w // 2, 0)
            return float(w) / cnt.astype(F32)
        return jnp.concatenate([mean_w[:edge] * rescale(t_first), mean_w[edge:tm - edge],
                                mean_w[tm - edge:] * rescale(t_last)], axis=0)

    ahead = lambda a, d: pltpu.roll(a, rows - d, 0)
    behind = lambda a, d: pltpu.roll(a, d, 0)

    def gated_out(j, mixed_j):
        cols = slice(j * POOL_GROUP_WIDTH, (j + 1) * POOL_GROUP_WIDTH)
        z = (mixed_j * sc_ref[:, cols] * _silu(gate[:, cols])).astype(BF16)
        return jnp.dot(z, wout_ref[cols, :], preferred_element_type=F32)

    out = None
    pending = None
    for j, w in enumerate(POOL_WINDOWS):
        ug = u[:, j * POOL_GROUP_WIDTH:(j + 1) * POOL_GROUP_WIDTH]
        run, span = ug, 1
        while 2 * span < w:
            run, span = run + ahead(run, span), 2 * span
        win = behind(run, span) + run
        pooled = clipped_window_fix(win[blk:blk + tm] * (1.0 / w), w)
        mix = (pooled - ug[blk:blk + tm]).astype(BF16)
        mixed_j = jnp.dot(mix, wg_ref[j], preferred_element_type=F32)
        if pending is not None:
            part = gated_out(*pending)
            out = part if out is None else out + part
        pending = (j, mixed_j)
    out = out + gated_out(*pending)
    y_ref[0] = x + mod1_ref[2, 0] * out


def _out_proj_and_pool_layer(o, sg, x, mod0, mod1, w_attn_out, norm_g, w_in, w_group, scale, w_out):
    b, s, _ = x.shape
    tm = min(ROW_TILE, s)
    blk = HALO_BLOCK
    per = tm // blk
    last = s // blk - 1
    prev = lambda i: jnp.maximum(i * per - 1, 0)
    nxt = lambda i: jnp.minimum((i + 1) * per, last)
    mod_spec = pl.BlockSpec((3, 1, 1, D_MODEL), lambda bi, i: (0, bi, 0, 0))

    def rows_specs(width):
        return [pl.BlockSpec((1, blk, width), lambda bi, i: (bi, prev(i), 0)),
                pl.BlockSpec((1, tm, width), lambda bi, i: (bi, i, 0)),
                pl.BlockSpec((1, blk, width), lambda bi, i: (bi, nxt(i), 0))]

    kern = functools.partial(_out_pool_kernel, tm=tm, seq=s)
    return pl.pallas_call(
        kern,
        grid=(b, s // tm),
        in_specs=[
            pl.BlockSpec((1, N_HEADS, blk, HEAD_DIM), lambda bi, i: (bi, 0, prev(i), 0)),
            pl.BlockSpec((1, N_HEADS, tm, HEAD_DIM), lambda bi, i: (bi, 0, i, 0)),
            pl.BlockSpec((1, N_HEADS, blk, HEAD_DIM), lambda bi, i: (bi, 0, nxt(i), 0)),
            *rows_specs(Q_DIM),
            *rows_specs(D_MODEL),
            mod_spec,
            mod_spec,
            _resident_spec((Q_DIM, D_MODEL)),
            _resident_spec((1, D_MODEL)),
            _resident_spec((D_MODEL, 2 * POOL_WIDTH)),
            _resident_spec((N_POOL_GROUPS, POOL_GROUP_WIDTH, POOL_GROUP_WIDTH)),
            _resident_spec((1, POOL_WIDTH)),
            _resident_spec((POOL_WIDTH, D_MODEL)),
        ],
        out_specs=pl.BlockSpec((1, tm, D_MODEL), lambda bi, i: (bi, i, 0)),
        out_shape=jax.ShapeDtypeStruct((b, s, D_MODEL), F32),
        compiler_params=_compiler_params(2),
        name="out_proj_pool_layer",
    )(o, o, o, sg, sg, sg, x, x, x, mod0, mod1, w_attn_out, norm_g, w_in, w_group, scale, w_out)


def _rope_tables(s):
    rows = s // GRID_W
    inv = ROPE_THETA ** (-jnp.arange(0, AXIS_DIM, 2, dtype=F32) / AXIS_DIM)
    n_freq = inv.shape[0]
    row_ang = jnp.arange(rows, dtype=F32)[:, None] * inv
    col_ang = jnp.arange(GRID_W, dtype=F32)[:, None] * inv

    def on_grid(fn):
        r = jnp.broadcast_to(fn(row_ang)[:, None, :], (rows, GRID_W, n_freq))
        c = jnp.broadcast_to(fn(col_ang)[None, :, :], (rows, GRID_W, n_freq))
        return jnp.concatenate([r, c], axis=-1).reshape(s, 2 * n_freq)

    c, sn = on_grid(jnp.cos), on_grid(jnp.sin)
    return jnp.concatenate([c, c], axis=-1), jnp.concatenate([-sn, sn], axis=-1)


def _gained_rope_tables(cos_t, sin_t, gain):
    half = HEAD_DIM // 2
    partner_gain = jnp.concatenate([gain[half:], gain[:half]])
    return cos_t * gain, sin_t * partner_gain


def _logit_bound(q_gain, k_gain):
    return (HEAD_DIM ** 0.5 * 1.01) * jnp.max(jnp.abs(q_gain)) * jnp.max(jnp.abs(k_gain))


def _deinterleave(a, n_heads):
    lead = a.shape[:-1]
    a = a.reshape(lead + (n_heads, HEAD_DIM // 2, 2))
    a = jnp.concatenate([a[..., 0], a[..., 1]], axis=-1)
    return a.reshape(lead + (n_heads * HEAD_DIM,))


def _trunk(x, mod, params):
    (norm_g, w_t, w_kg, rope_q, rope_k, small_logits, attn_w_out,
     pool_w_in, pool_w_group, pool_scale, pool_w_out) = params
    qt, k, vt, sg = _attn_in_proj(x, mod[0], norm_g[0:1], w_t, w_kg, rope_q, rope_k)
    o = lax.cond(small_logits,
                 functools.partial(_flash_attention, running_max=False),
                 functools.partial(_flash_attention, running_max=True),
                 qt, k, vt)
    return _out_proj_and_pool_layer(o, sg, x, mod[0], mod[1], attn_w_out, norm_g[1:2],
                                    pool_w_in, pool_w_group, pool_scale, pool_w_out)


def kernel(x_prompt, x_sample, c_prompt, c_sample, norm_g, ada_w, ada_b, attn_w_in, attn_q_norm,
           attn_k_norm, attn_w_out, pool_w_in, pool_w_group, pool_scale, pool_w_out):
    assert DEPTH == 2 and attn_w_in.shape[0] == 1 and pool_w_in.shape[0] == 1
    bp, bs = x_prompt.shape[0], x_sample.shape[0]
    s = x_prompt.shape[1]
    assert x_sample.shape[1] == s and s % ROW_TILE == 0 and s % KV_TILE == 0

    c_all = jnp.concatenate([c_prompt, c_sample], axis=0)
    pad = (-c_all.shape[0]) % V7X_SUBLANES
    c_pad = jnp.pad(c_all, ((0, pad), (0, 0)))
    mod = _modulation(c_pad, ada_w, ada_b)
    mod = mod.reshape(DEPTH, 3, c_pad.shape[0], 1, D_MODEL)

    w_in = attn_w_in[0]
    w_q = _deinterleave(w_in[:, :Q_DIM], N_HEADS)
    w_k = _deinterleave(w_in[:, Q_DIM:Q_DIM + KV_DIM], N_KV_HEADS)
    w_v = w_in[:, Q_DIM + KV_DIM:Q_DIM + 2 * KV_DIM]
    w_t = jnp.concatenate([w_q, w_v], axis=1).T.astype(BF16)
    w_kg = jnp.concatenate([w_k, w_in[:, Q_DIM + 2 * KV_DIM:]], axis=1).astype(BF16)
    cos_t, sin_t = _rope_tables(s)
    gq = _deinterleave(attn_q_norm[0], 1) * (HEAD_DIM ** -0.5 * LOG2_E)
    rope_q = tuple(t.T for t in _gained_rope_tables(cos_t, sin_t, gq))
    rope_k = _gained_rope_tables(cos_t, sin_t, _deinterleave(attn_k_norm[0], 1))
    small_logits = _logit_bound(attn_q_norm[0], attn_k_norm[0]) <= MAX_UNSHIFTED_LOGIT
    params = (norm_g, w_t, w_kg, rope_q, rope_k, small_logits, attn_w_out[0].astype(BF16),
              pool_w_in[0].astype(BF16), pool_w_group[0].astype(BF16),
              pool_scale[0].reshape(1, POOL_WIDTH), pool_w_out[0].astype(BF16))

    y_prompt = _trunk(x_prompt, mod[:, :, :bp], params)
    y_sample = _trunk(x_sample, mod[:, :, bp:bp + bs], params)
    return (y_prompt, y_sample)
```

```python
import functools

import jax
import jax.numpy as jnp
from jax import lax
from jax.experimental import pallas as pl
from jax.experimental.pallas import tpu as pltpu

D_MODEL = 1024
DEPTH = 2
N_HEADS = 8
N_KV_HEADS = 2
HEAD_DIM = 128
Q_DIM = N_HEADS * HEAD_DIM
KV_DIM = N_KV_HEADS * HEAD_DIM
AXIS_DIM = HEAD_DIM // 2
ROPE_THETA = 10000.0
GRID_W = 64
POOL_WINDOWS = (2, 4, 8, 16)
N_POOL_GROUPS = len(POOL_WINDOWS)
POOL_WIDTH = D_MODEL
POOL_GROUP_WIDTH = POOL_WIDTH // N_POOL_GROUPS
RMS_EPS = 1e-6
LOG2_E = 1.4426950408889634
MAX_UNSHIFTED_LOGIT = 40.0

V7X_SUBLANES = 8
V7X_LANES = 128
V7X_VMEM_BYTES = 64 * 1024 * 1024
VMEM_LIMIT_BYTES = (V7X_VMEM_BYTES * 7) // 8

ROW_TILE = 1024
IN_PROJ_SLAB_ROWS = 256
IN_PROJ_T_SLAB_ROWS = 4 * HEAD_DIM
Q_TILE = 4096
KV_TILE = 512
POOL_HALO = max(POOL_WINDOWS) // 2
HALO_BLOCK = 16
assert POOL_HALO <= V7X_SUBLANES and 2 * POOL_HALO <= HALO_BLOCK

BF16 = jnp.bfloat16
F32 = jnp.float32


def _compiler_params(n_grid_axes):
    return pltpu.CompilerParams(
        dimension_semantics=("arbitrary",) * n_grid_axes,
        vmem_limit_bytes=VMEM_LIMIT_BYTES,
    )


def _resident_spec(shape):
    return pl.BlockSpec(shape, lambda *_: (0,) * len(shape), pipeline_mode=pl.Buffered(1))


def _silu(x):
    h = 0.5 * x
    return h + h * jnp.tanh(h)


def _modulated_rmsnorm(x, norm_g, scl, shift):
    gain = norm_g * (1.0 + scl)
    ms = jnp.mean(x * x, axis=-1, keepdims=True)
    return (x * lax.rsqrt(ms + RMS_EPS)) * gain + shift


def _mod_kernel(c_ref, w_ref, b_ref, o_ref):
    cs = _silu(c_ref[...])
    w = w_ref[0]
    cs_hi = cs.astype(BF16)
    cs_lo = (cs - cs_hi.astype(F32)).astype(BF16)
    w_hi = w.astype(BF16)
    w_lo = (w - w_hi.astype(F32)).astype(BF16)
    dot = functools.partial(jnp.dot, preferred_element_type=F32)
    o_ref[0, 0] = dot(cs_hi, w_hi) + (dot(cs_lo, w_hi) + dot(cs_hi, w_lo)) + b_ref[0, 0]


def _modulation(c_pad, ada_w, ada_b):
    bp = c_pad.shape[0]
    return pl.pallas_call(
        _mod_kernel,
        grid=(DEPTH, 3),
        in_specs=[
            pl.BlockSpec((bp, D_MODEL), lambda i, j: (0, 0)),
            pl.BlockSpec((1, D_MODEL, D_MODEL), lambda i, j: (i, 0, j)),
            pl.BlockSpec((1, 1, 1, D_MODEL), lambda i, j: (i, j, 0, 0)),
        ],
        out_specs=pl.BlockSpec((1, 1, bp, D_MODEL), lambda i, j: (i, j, 0, 0)),
        out_shape=jax.ShapeDtypeStruct((DEPTH, 3, bp, D_MODEL), F32),
        compiler_params=_compiler_params(2),
        name="adaln_modulation",
    )(c_pad, ada_w, ada_b.reshape(DEPTH, 3, 1, D_MODEL))


def _attn_in_kernel(x_ref, mod_ref, ng_ref, wt_ref, w_ref, qa_ref, qb_ref, ka_ref, kb_ref,
                    qt_ref, k_ref, vt_ref, sg_ref):
    tm = x_ref.shape[1]
    half = HEAD_DIM // 2
    slab = IN_PROJ_SLAB_ROWS
    hs, ys = [], []
    for r in range(tm // slab):
        rows = pl.ds(r * slab, slab)
        hs.append(_modulated_rmsnorm(x_ref[0, rows], ng_ref[...], mod_ref[1, 0], mod_ref[0, 0]).astype(BF16))
        ys.append(jnp.dot(hs[-1], w_ref[...], preferred_element_type=F32))
    for r, y in enumerate(ys):
        rows = pl.ds(r * slab, slab)
        for j in range(N_KV_HEADS):
            t = y[:, j * HEAD_DIM:(j + 1) * HEAD_DIM]
            rs = lax.rsqrt(jnp.mean(t * t, axis=1, keepdims=True) + RMS_EPS)
            k_ref[0, j, rows] = ((t * ka_ref[rows] + pltpu.roll(t, half, 1) * kb_ref[rows]) * rs).astype(BF16)
        sg_ref[0, rows] = _silu(y[:, KV_DIM:]).astype(BF16)

    h_t = jnp.concatenate(hs, axis=0).T
    qa = qa_ref[...]
    qb = qb_ref[...]
    n_t_rows = Q_DIM + KV_DIM

    def project_t(c):
        lo = c * IN_PROJ_T_SLAB_ROWS
        return jnp.dot(wt_ref[lo:min(lo + IN_PROJ_T_SLAB_ROWS, n_t_rows), :], h_t, preferred_element_type=F32)

    def finish_t(c, yt):
        for i in range(yt.shape[0] // HEAD_DIM):
            head = c * (IN_PROJ_T_SLAB_ROWS // HEAD_DIM) + i
            t = yt[i * HEAD_DIM:(i + 1) * HEAD_DIM]
            if head < N_HEADS:
                rs = lax.rsqrt(jnp.mean(t * t, axis=0, keepdims=True) + RMS_EPS)
                partner = jnp.concatenate([t[half:], t[:half]], axis=0)
                qt_ref[0, head] = ((t * qa + partner * qb) * rs).astype(BF16)
            else:
                for cc in range(tm // KV_TILE):
                    vt_ref[0, head - N_HEADS, cc] = t[:, cc * KV_TILE:(cc + 1) * KV_TILE].astype(BF16)

    n_t_slabs = pl.cdiv(n_t_rows, IN_PROJ_T_SLAB_ROWS)
    yt_next = project_t(0)
    for c in range(n_t_slabs):
        yt = yt_next
        if c + 1 < n_t_slabs:
            yt_next = project_t(c + 1)
        finish_t(c, yt)


def _attn_in_proj(x, mod, norm_g, w_t, w_kg, rope_q, rope_k):
    b, s, _ = x.shape
    tm = min(ROW_TILE, s)
    n_chunks = tm // KV_TILE
    return pl.pallas_call(
        _attn_in_kernel,
        grid=(b, s // tm),
        in_specs=[
            pl.BlockSpec((1, tm, D_MODEL), lambda bi, i: (bi, i, 0)),
            pl.BlockSpec((3, 1, 1, D_MODEL), lambda bi, i: (0, bi, 0, 0)),
            _resident_spec((1, D_MODEL)),
            _resident_spec((Q_DIM + KV_DIM, D_MODEL)),
            _resident_spec((D_MODEL, KV_DIM + Q_DIM)),
            pl.BlockSpec((HEAD_DIM, tm), lambda bi, i: (0, i)),
            pl.BlockSpec((HEAD_DIM, tm), lambda bi, i: (0, i)),
            pl.BlockSpec((tm, HEAD_DIM), lambda bi, i: (i, 0)),
            pl.BlockSpec((tm, HEAD_DIM), lambda bi, i: (i, 0)),
        ],
        out_specs=[
            pl.BlockSpec((1, N_HEADS, HEAD_DIM, tm), lambda bi, i: (bi, 0, 0, i)),
            pl.BlockSpec((1, N_KV_HEADS, tm, HEAD_DIM), lambda bi, i: (bi, 0, i, 0)),
            pl.BlockSpec((1, N_KV_HEADS, n_chunks, HEAD_DIM, KV_TILE), lambda bi, i: (bi, 0, i, 0, 0)),
            pl.BlockSpec((1, tm, Q_DIM), lambda bi, i: (bi, i, 0)),
        ],
        out_shape=[
            jax.ShapeDtypeStruct((b, N_HEADS, HEAD_DIM, s), BF16),
            jax.ShapeDtypeStruct((b, N_KV_HEADS, s, HEAD_DIM), BF16),
            jax.ShapeDtypeStruct((b, N_KV_HEADS, s // KV_TILE, HEAD_DIM, KV_TILE), BF16),
            jax.ShapeDtypeStruct((b, s, Q_DIM), BF16),
        ],
        compiler_params=_compiler_params(2),
        name="attn_in_proj",
    )(x, mod, norm_g, w_t, w_kg, *rope_q, *rope_k)


def _flash_kernel(qt_ref, k_ref, vt_ref, o_ref, *scratch, n_kv, tk, running_max):
    q_t = qt_ref[0, 0]
    tq = q_t.shape[1]

    def scores_t(j):
        return jnp.dot(k_ref[0, 0, pl.ds(j * tk, tk), :], q_t, preferred_element_type=F32)

    if running_max:
        m_sc, l_sc, acc_sc = scratch
        m_sc[...] = jnp.full(m_sc.shape, -jnp.inf, F32)
        l_sc[...] = jnp.zeros(l_sc.shape, F32)
        acc_sc[...] = jnp.zeros(acc_sc.shape, F32)

        def body(j, carry):
            s_t = scores_t(j)
            m_prev = m_sc[...]
            m_next = jnp.maximum(m_prev, jnp.max(s_t, axis=0, keepdims=True))
            alpha = jnp.exp2(m_prev - m_next)
            p_t = jnp.exp2(s_t - m_next)
            l_sc[...] = alpha * l_sc[...] + jnp.sum(p_t, axis=0, keepdims=True)
            acc_sc[...] = acc_sc[...] * alpha + jnp.dot(
                vt_ref[0, 0, j], p_t.astype(BF16), preferred_element_type=F32)
            m_sc[...] = m_next
            return carry

        lax.fori_loop(0, n_kv, body, 0)
        o_t = acc_sc[...] / l_sc[...]
    else:
        acc = jnp.zeros((HEAD_DIM, tq), F32)
        l_part = jnp.zeros((V7X_SUBLANES, tq), F32)
        s_next = scores_t(0)
        for j in range(n_kv):
            s_t = s_next
            if j + 1 < n_kv:
                s_next = scores_t(j + 1)
            p_t = jnp.exp2(s_t)
            l_part = l_part + jnp.sum(p_t.reshape(tk // V7X_SUBLANES, V7X_SUBLANES, tq), axis=0)
            acc = acc + jnp.dot(vt_ref[0, 0, j], p_t.astype(BF16), preferred_element_type=F32)
        o_t = acc / jnp.sum(l_part, axis=0, keepdims=True)
    o_ref[0, 0] = o_t.T.astype(BF16)


def _flash_attention(qt, k, vt, *, running_max):
    b, nh, dh, s = qt.shape
    nkv, n_kv, tk = vt.shape[1], vt.shape[2], vt.shape[4]
    group = nh // nkv
    tq = min(Q_TILE, s)
    assert s % tq == 0
    n_q = s // tq
    kern = functools.partial(_flash_kernel, n_kv=n_kv, tk=tk, running_max=running_max)
    stat = pltpu.VMEM((1, tq), F32)
    return pl.pallas_call(
        kern,
        grid=(b, nkv, group * n_q),
        in_specs=[
            pl.BlockSpec((1, 1, dh, tq), lambda bi, g, i: (bi, g * group + i // n_q, 0, i % n_q)),
            pl.BlockSpec((1, 1, s, dh), lambda bi, g, i: (bi, g, 0, 0)),
            pl.BlockSpec((1, 1, n_kv, dh, tk), lambda bi, g, i: (bi, g, 0, 0, 0)),
        ],
        out_specs=pl.BlockSpec((1, 1, tq, dh), lambda bi, g, i: (bi, g * group + i // n_q, i % n_q, 0)),
        out_shape=jax.ShapeDtypeStruct((b, nh, s, dh), BF16),
        scratch_shapes=[stat, stat, pltpu.VMEM((dh, tq), F32)] if running_max else [],
        compiler_params=_compiler_params(3),
        name="flash_attention_online" if running_max else "flash_attention",
    )(qt, k, vt)


def _out_pool_kernel(op_ref, o_ref, on_ref, sgp_ref, sg_ref, sgn_ref, xp_ref, x_ref, xn_ref,
                     mod0_ref, mod1_ref, wao_ref, ng_ref, win_ref, wg_ref, sc_ref, wout_ref, y_ref,
                     *, tm, seq):
    blk = HALO_BLOCK
    rows = tm + 2 * blk
    i = pl.program_id(1)
    n_tiles = pl.num_programs(1)

    def with_halo(prev, main, nxt):
        return jnp.concatenate([prev, main, nxt], axis=0)

    heads = lambda ref: jnp.concatenate([ref[0, hh] for hh in range(N_HEADS)], axis=1)
    o = with_halo(heads(op_ref), heads(o_ref), heads(on_ref))
    sg = with_halo(sgp_ref[0], sg_ref[0], sgn_ref[0])
    x0 = with_halo(xp_ref[0], x_ref[0], xn_ref[0])
    gated = o * sg
    halves = ((0, blk + tm // 2), (blk + tm // 2, rows))
    x1_parts = [x0[lo:hi] + mod0_ref[2, 0] * jnp.dot(gated[lo:hi], wao_ref[...], preferred_element_type=F32)
                for lo, hi in halves]
    y_parts = [jnp.dot(_modulated_rmsnorm(part, ng_ref[...], mod1_ref[1, 0], mod1_ref[0, 0]).astype(BF16),
                       win_ref[...], preferred_element_type=F32) for part in x1_parts]
    x = jnp.concatenate(x1_parts, axis=0)[blk:blk + tm]
    y = jnp.concatenate(y_parts, axis=0)
    u = y[:, :POOL_WIDTH]
    u = jnp.concatenate([jnp.where(i > 0, u[:blk], 0.0),
                         u[blk:blk + tm],
                         jnp.where(i < n_tiles - 1, u[blk + tm:], 0.0)], axis=0)
    gate = y[blk:blk + tm, POOL_WIDTH:]

    edge = V7X_SUBLANES
    t_first = lax.broadcasted_iota(jnp.int32, (edge, 1), 0) + i * tm
    t_last = t_first + (tm - edge)

    def clipped_window_fix(mean_w, w):
        def rescale(t):
            cnt = jnp.minimum(t + w // 2, seq) - jnp.maximum(t - w // 2, 0)
            return float(w) / cnt.astype(F32)
        return jnp.concatenate([mean_w[:edge] * rescale(t_first), mean_w[edge:tm - edge],
                                mean_w[tm - edge:] * rescale(t_last)], axis=0)

    ahead = lambda a, d: pltpu.roll(a, rows - d, 0)
    behind = lambda a, d: pltpu.roll(a, d, 0)

    def gated_out(j, mixed_j):
        cols = slice(j * POOL_GROUP_WIDTH, (j + 1) * POOL_GROUP_WIDTH)
        z = (mixed_j * sc_ref[:, cols] * _silu(gate[:, cols])).astype(BF16)
        return jnp.dot(z, wout_ref[cols, :], preferred_element_type=F32)

    out = None
    pending = None
    for j, w in enumerate(POOL_WINDOWS):
        ug = u[:, j * POOL_GROUP_WIDTH:(j + 1) * POOL_GROUP_WIDTH]
        run, span = ug, 1
        while 2 * span < w:
            run, span = run + ahead(run, span), 2 * span
        win = behind(run, span) + run
        pooled = clipped_window_fix(win[blk:blk + tm] * (1.0 / w), w)
        mix = (pooled - ug[blk:blk + tm]).astype(BF16)
        mixed_j = jnp.dot(mix, wg_ref[j], preferred_element_type=F32)
        if pending is not None:
            part = gated_out(*pending)
            out = part if out is None else out + part
        pending = (j, mixed_j)
    out = out + gated_out(*pending)
    y_ref[0] = x + mod1_ref[2, 0] * out


def _out_proj_and_pool_layer(o, sg, x, mod0, mod1, w_attn_out, norm_g, w_in, w_group, scale, w_out):
    b, s, _ = x.shape
    tm = min(ROW_TILE, s)
    blk = HALO_BLOCK
    per = tm // blk
    last = s // blk - 1
    prev = lambda i: jnp.maximum(i * per - 1, 0)
    nxt = lambda i: jnp.minimum((i + 1) * per, last)
    mod_spec = pl.BlockSpec((3, 1, 1, D_MODEL), lambda bi, i: (0, bi, 0, 0))

    def rows_specs(width):
        return [pl.BlockSpec((1, blk, width), lambda bi, i: (bi, prev(i), 0)),
                pl.BlockSpec((1, tm, width), lambda bi, i: (bi, i, 0)),
                pl.BlockSpec((1, blk, width), lambda bi, i: (bi, nxt(i), 0))]

    kern = functools.partial(_out_pool_kernel, tm=tm, seq=s)
    return pl.pallas_call(
        kern,
        grid=(b, s // tm),
        in_specs=[
            pl.BlockSpec((1, N_HEADS, blk, HEAD_DIM), lambda bi, i: (bi, 0, prev(i), 0)),
            pl.BlockSpec((1, N_HEADS, tm, HEAD_DIM), lambda bi, i: (bi, 0, i, 0)),
            pl.BlockSpec((1, N_HEADS, blk, HEAD_DIM), lambda bi, i: (bi, 0, nxt(i), 0)),
            *rows_specs(Q_DIM),
            *rows_specs(D_MODEL),
            mod_spec,
            mod_spec,
            _resident_spec((Q_DIM, D_MODEL)),
            _resident_spec((1, D_MODEL)),
            _resident_spec((D_MODEL, 2 * POOL_WIDTH)),
            _resident_spec((N_POOL_GROUPS, POOL_GROUP_WIDTH, POOL_GROUP_WIDTH)),
            _resident_spec((1, POOL_WIDTH)),
            _resident_spec((POOL_WIDTH, D_MODEL)),
        ],
        out_specs=pl.BlockSpec((1, tm, D_MODEL), lambda bi, i: (bi, i, 0)),
        out_shape=jax.ShapeDtypeStruct((b, s, D_MODEL), F32),
        compiler_params=_compiler_params(2),
        name="out_proj_pool_layer",
    )(o, o, o, sg, sg, sg, x, x, x, mod0, mod1, w_attn_out, norm_g, w_in, w_group, scale, w_out)


def _rope_tables(s):
    rows = s // GRID_W
    inv = ROPE_THETA ** (-jnp.arange(0, AXIS_DIM, 2, dtype=F32) / AXIS_DIM)
    n_freq = inv.shape[0]
    row_ang = jnp.arange(rows, dtype=F32)[:, None] * inv
    col_ang = jnp.arange(GRID_W, dtype=F32)[:, None] * inv

    def on_grid(fn):
        r = jnp.broadcast_to(fn(row_ang)[:, None, :], (rows, GRID_W, n_freq))
        c = jnp.broadcast_to(fn(col_ang)[None, :, :], (rows, GRID_W, n_freq))
        return jnp.concatenate([r, c], axis=-1).reshape(s, 2 * n_freq)

    c, sn = on_grid(jnp.cos), on_grid(jnp.sin)
    return jnp.concatenate([c, c], axis=-1), jnp.concatenate([-sn, sn], axis=-1)


def _gained_rope_tables(cos_t, sin_t, gain):
    half = HEAD_DIM // 2
    partner_gain = jnp.concatenate([gain[half:], gain[:half]])
    return cos_t * gain, sin_t * partner_gain


def _logit_bound(q_gain, k_gain):
    return (HEAD_DIM ** 0.5 * 1.01) * jnp.max(jnp.abs(q_gain)) * jnp.max(jnp.abs(k_gain))


def _deinterleave(a, n_heads):
    lead = a.shape[:-1]
    a = a.reshape(lead + (n_heads, HEAD_DIM // 2, 2))
    a = jnp.concatenate([a[..., 0], a[..., 1]], axis=-1)
    return a.reshape(lead + (n_heads * HEAD_DIM,))


def _trunk(x, mod, params):
    (norm_g, w_t, w_kg, rope_q, rope_k, small_logits, attn_w_out,
     pool_w_in, pool_w_group, pool_scale, pool_w_out) = params
    qt, k, vt, sg = _attn_in_proj(x, mod[0], norm_g[0:1], w_t, w_kg, rope_q, rope_k)
    o = lax.cond(small_logits,
                 functools.partial(_flash_attention, running_max=False),
                 functools.partial(_flash_attention, running_max=True),
                 qt, k, vt)
    return _out_proj_and_pool_layer(o, sg, x, mod[0], mod[1], attn_w_out, norm_g[1:2],
                                    pool_w_in, pool_w_group, pool_scale, pool_w_out)


def kernel(x_prompt, x_sample, c_prompt, c_sample, norm_g, ada_w, ada_b, attn_w_in, attn_q_norm,
           attn_k_norm, attn_w_out, pool_w_in, pool_w_group, pool_scale, pool_w_out):
    assert DEPTH == 2 and attn_w_in.shape[0] == 1 and pool_w_in.shape[0] == 1
    bp, bs = x_prompt.shape[0], x_sample.shape[0]
    s = x_prompt.shape[1]
    assert x_sample.shape[1] == s and s % ROW_TILE == 0 and s % KV_TILE == 0

    c_all = jnp.concatenate([c_prompt, c_sample], axis=0)
    pad = (-c_all.shape[0]) % V7X_SUBLANES
    c_pad = jnp.pad(c_all, ((0, pad), (0, 0)))
    mod = _modulation(c_pad, ada_w, ada_b)
    mod = mod.reshape(DEPTH, 3, c_pad.shape[0], 1, D_MODEL)

    w_in = attn_w_in[0]
    w_q = _deinterleave(w_in[:, :Q_DIM], N_HEADS)
    w_k = _deinterleave(w_in[:, Q_DIM:Q_DIM + KV_DIM], N_KV_HEADS)
    w_v = w_in[:, Q_DIM + KV_DIM:Q_DIM + 2 * KV_DIM]
    w_t = jnp.concatenate([w_q, w_v], axis=1).T.astype(BF16)
    w_kg = jnp.concatenate([w_k, w_in[:, Q_DIM + 2 * KV_DIM:]], axis=1).astype(BF16)
    cos_t, sin_t = _rope_tables(s)
    gq = _deinterleave(attn_q_norm[0], 1) * (HEAD_DIM ** -0.5 * LOG2_E)
    rope_q = tuple(t.T for t in _gained_rope_tables(cos_t, sin_t, gq))
    rope_k = _gained_rope_tables(cos_t, sin_t, _deinterleave(attn_k_norm[0], 1))
    small_logits = _logit_bound(attn_q_norm[0], attn_k_norm[0]) <= MAX_UNSHIFTED_LOGIT
    params = (norm_g, w_t, w_kg, rope_q, rope_k, small_logits, attn_w_out[0].astype(BF16),
              pool_w_in[0].astype(BF16), pool_w_group[0].astype(BF16),
              pool_scale[0].reshape(1, POOL_WIDTH), pool_w_out[0].astype(BF16))

    y_prompt = _trunk(x_prompt, mod[:, :, :bp], params)
    y_sample = _trunk(x_sample, mod[:, :, bp:bp + bs], params)
    return (y_prompt, y_sample)
```

```python
import functools

import jax
import jax.numpy as jnp
from jax import lax
from jax.experimental import pallas as pl
from jax.experimental.pallas import tpu as pltpu

D_MODEL = 1024
DEPTH = 2
N_HEADS = 8
N_KV_HEADS = 2
HEAD_DIM = 128
Q_DIM = N_HEADS * HEAD_DIM
KV_DIM = N_KV_HEADS * HEAD_DIM
AXIS_DIM = HEAD_DIM // 2
ROPE_THETA = 10000.0
GRID_W = 64
POOL_WINDOWS = (2, 4, 8, 16)
N_POOL_GROUPS = len(POOL_WINDOWS)
POOL_WIDTH = D_MODEL
POOL_GROUP_WIDTH = POOL_WIDTH // N_POOL_GROUPS
RMS_EPS = 1e-6
LOG2_E = 1.4426950408889634
MAX_UNSHIFTED_LOGIT = 40.0

V7X_SUBLANES = 8
V7X_LANES = 128
V7X_VMEM_BYTES = 64 * 1024 * 1024
VMEM_LIMIT_BYTES = (V7X_VMEM_BYTES * 7) // 8

ROW_TILE = 1024
IN_PROJ_SLAB_ROWS = 256
IN_PROJ_T_SLAB_ROWS = 5 * HEAD_DIM
Q_TILE = 4096
KV_TILE = 512
POOL_HALO = max(POOL_WINDOWS) // 2
HALO_BLOCK = 16
assert POOL_HALO <= V7X_SUBLANES and 2 * POOL_HALO <= HALO_BLOCK

BF16 = jnp.bfloat16
F32 = jnp.float32


def _compiler_params(n_grid_axes):
    return pltpu.CompilerParams(
        dimension_semantics=("arbitrary",) * n_grid_axes,
        vmem_limit_bytes=VMEM_LIMIT_BYTES,
    )


def _resident_spec(shape):
    return pl.BlockSpec(shape, lambda *_: (0,) * len(shape), pipeline_mode=pl.Buffered(1))


def _silu(x):
    h = 0.5 * x
    return h + h * jnp.tanh(h)


def _modulated_rmsnorm(x, norm_g, scl, shift):
    gain = norm_g * (1.0 + scl)
    ms = jnp.mean(x * x, axis=-1, keepdims=True)
    return (x * lax.rsqrt(ms + RMS_EPS)) * gain + shift


def _mod_kernel(c_ref, w_ref, b_ref, o_ref):
    cs = _silu(c_ref[...])
    w = w_ref[0]
    cs_hi = cs.astype(BF16)
    cs_lo = (cs - cs_hi.astype(F32)).astype(BF16)
    w_hi = w.astype(BF16)
    w_lo = (w - w_hi.astype(F32)).astype(BF16)
    dot = functools.partial(jnp.dot, preferred_element_type=F32)
    o_ref[0, 0] = dot(cs_hi, w_hi) + (dot(cs_lo, w_hi) + dot(cs_hi, w_lo)) + b_ref[0, 0]


def _modulation(c_pad, ada_w, ada_b):
    bp = c_pad.shape[0]
    return pl.pallas_call(
        _mod_kernel,
        grid=(DEPTH, 3),
        in_specs=[
            pl.BlockSpec((bp, D_MODEL), lambda i, j: (0, 0)),
            pl.BlockSpec((1, D_MODEL, D_MODEL), lambda i, j: (i, 0, j)),
            pl.BlockSpec((1, 1, 1, D_MODEL), lambda i, j: (i, j, 0, 0)),
        ],
        out_specs=pl.BlockSpec((1, 1, bp, D_MODEL), lambda i, j: (i, j, 0, 0)),
        out_shape=jax.ShapeDtypeStruct((DEPTH, 3, bp, D_MODEL), F32),
        compiler_params=_compiler_params(2),
        name="adaln_modulation",
    )(c_pad, ada_w, ada_b.reshape(DEPTH, 3, 1, D_MODEL))


def _attn_in_kernel(x_ref, mod_ref, ng_ref, wt_ref, w_ref, qa_ref, qb_ref, ka_ref, kb_ref,
                    qt_ref, k_ref, vt_ref, sg_ref):
    tm = x_ref.shape[1]
    half = HEAD_DIM // 2
    slab = IN_PROJ_SLAB_ROWS
    hs, ys = [], []
    for r in range(tm // slab):
        rows = pl.ds(r * slab, slab)
        hs.append(_modulated_rmsnorm(x_ref[0, rows], ng_ref[...], mod_ref[1, 0], mod_ref[0, 0]).astype(BF16))
        ys.append(jnp.dot(hs[-1], w_ref[...], preferred_element_type=F32))
    for r, y in enumerate(ys):
        rows = pl.ds(r * slab, slab)
        for j in range(N_KV_HEADS):
            t = y[:, j * HEAD_DIM:(j + 1) * HEAD_DIM]
            rs = lax.rsqrt(jnp.mean(t * t, axis=1, keepdims=True) + RMS_EPS)
            k_ref[0, j, rows] = ((t * ka_ref[rows] + pltpu.roll(t, half, 1) * kb_ref[rows]) * rs).astype(BF16)
        sg_ref[0, rows] = _silu(y[:, KV_DIM:]).astype(BF16)

    h_t = jnp.concatenate(hs, axis=0).T
    qa = qa_ref[...]
    qb = qb_ref[...]
    n_t_rows = Q_DIM + KV_DIM

    def project_t(c):
        lo = c * IN_PROJ_T_SLAB_ROWS
        return jnp.dot(wt_ref[lo:min(lo + IN_PROJ_T_SLAB_ROWS, n_t_rows), :], h_t, preferred_element_type=F32)

    def finish_t(c, yt):
        for i in range(yt.shape[0] // HEAD_DIM):
            head = c * (IN_PROJ_T_SLAB_ROWS // HEAD_DIM) + i
            t = yt[i * HEAD_DIM:(i + 1) * HEAD_DIM]
            if head < N_HEADS:
                rs = lax.rsqrt(jnp.mean(t * t, axis=0, keepdims=True) + RMS_EPS)
                partner = jnp.concatenate([t[half:], t[:half]], axis=0)
                qt_ref[0, head] = ((t * qa + partner * qb) * rs).astype(BF16)
            else:
                for cc in range(tm // KV_TILE):
                    vt_ref[0, head - N_HEADS, cc] = t[:, cc * KV_TILE:(cc + 1) * KV_TILE].astype(BF16)

    n_t_slabs = pl.cdiv(n_t_rows, IN_PROJ_T_SLAB_ROWS)
    yt_next = project_t(0)
    for c in range(n_t_slabs):
        yt = yt_next
        if c + 1 < n_t_slabs:
            yt_next = project_t(c + 1)
        finish_t(c, yt)


def _attn_in_proj(x, mod, norm_g, w_t, w_kg, rope_q, rope_k):
    b, s, _ = x.shape
    tm = min(ROW_TILE, s)
    n_chunks = tm // KV_TILE
    return pl.pallas_call(
        _attn_in_kernel,
        grid=(b, s // tm),
        in_specs=[
            pl.BlockSpec((1, tm, D_MODEL), lambda bi, i: (bi, i, 0)),
            pl.BlockSpec((3, 1, 1, D_MODEL), lambda bi, i: (0, bi, 0, 0)),
            _resident_spec((1, D_MODEL)),
            _resident_spec((Q_DIM + KV_DIM, D_MODEL)),
            _resident_spec((D_MODEL, KV_DIM + Q_DIM)),
            pl.BlockSpec((HEAD_DIM, tm), lambda bi, i: (0, i)),
            pl.BlockSpec((HEAD_DIM, tm), lambda bi, i: (0, i)),
            pl.BlockSpec((tm, HEAD_DIM), lambda bi, i: (i, 0)),
            pl.BlockSpec((tm, HEAD_DIM), lambda bi, i: (i, 0)),
        ],
        out_specs=[
            pl.BlockSpec((1, N_HEADS, HEAD_DIM, tm), lambda bi, i: (bi, 0, 0, i)),
            pl.BlockSpec((1, N_KV_HEADS, tm, HEAD_DIM), lambda bi, i: (bi, 0, i, 0)),
            pl.BlockSpec((1, N_KV_HEADS, n_chunks, HEAD_DIM, KV_TILE), lambda bi, i: (bi, 0, i, 0, 0)),
            pl.BlockSpec((1, tm, Q_DIM), lambda bi, i: (bi, i, 0)),
        ],
        out_shape=[
            jax.ShapeDtypeStruct((b, N_HEADS, HEAD_DIM, s), BF16),
            jax.ShapeDtypeStruct((b, N_KV_HEADS, s, HEAD_DIM), BF16),
            jax.ShapeDtypeStruct((b, N_KV_HEADS, s // KV_TILE, HEAD_DIM, KV_TILE), BF16),
            jax.ShapeDtypeStruct((b, s, Q_DIM), BF16),
        ],
        compiler_params=_compiler_params(2),
        name="attn_in_proj",
    )(x, mod, norm_g, w_t, w_kg, *rope_q, *rope_k)


def _flash_kernel(qt_ref, k_ref, vt_ref, o_ref, *scratch, n_kv, tk, running_max):
    q_t = qt_ref[0, 0]
    tq = q_t.shape[1]

    def scores_t(j):
        return jnp.dot(k_ref[0, 0, pl.ds(j * tk, tk), :], q_t, preferred_element_type=F32)

    if running_max:
        m_sc, l_sc, acc_sc = scratch
        m_sc[...] = jnp.full(m_sc.shape, -jnp.inf, F32)
        l_sc[...] = jnp.zeros(l_sc.shape, F32)
        acc_sc[...] = jnp.zeros(acc_sc.shape, F32)

        def body(j, carry):
            s_t = scores_t(j)
            m_prev = m_sc[...]
            m_next = jnp.maximum(m_prev, jnp.max(s_t, axis=0, keepdims=True))
            alpha = jnp.exp2(m_prev - m_next)
            p_t = jnp.exp2(s_t - m_next)
            l_sc[...] = alpha * l_sc[...] + jnp.sum(p_t, axis=0, keepdims=True)
            acc_sc[...] = acc_sc[...] * alpha + jnp.dot(
                vt_ref[0, 0, j], p_t.astype(BF16), preferred_element_type=F32)
            m_sc[...] = m_next
            return carry

        lax.fori_loop(0, n_kv, body, 0)
        o_t = acc_sc[...] / l_sc[...]
    else:
        acc = jnp.zeros((HEAD_DIM, tq), F32)
        l_part = jnp.zeros((V7X_SUBLANES, tq), F32)
        s_next = scores_t(0)
        for j in range(n_kv):
            s_t = s_next
            if j + 1 < n_kv:
                s_next = scores_t(j + 1)
            p_t = jnp.exp2(s_t)
            l_part = l_part + jnp.sum(p_t.reshape(tk // V7X_SUBLANES, V7X_SUBLANES, tq), axis=0)
            acc = acc + jnp.dot(vt_ref[0, 0, j], p_t.astype(BF16), preferred_element_type=F32)
        o_t = acc / jnp.sum(l_part, axis=0, keepdims=True)
    o_ref[0, 0] = o_t.T.astype(BF16)


def _flash_attention(qt, k, vt, *, running_max):
    b, nh, dh, s = qt.shape
    nkv, n_kv, tk = vt.shape[1], vt.shape[2], vt.shape[4]
    group = nh // nkv
    tq = min(Q_TILE, s)
    assert s % tq == 0
    n_q = s // tq
    kern = functools.partial(_flash_kernel, n_kv=n_kv, tk=tk, running_max=running_max)
    stat = pltpu.VMEM((1, tq), F32)
    return pl.pallas_call(
        kern,
        grid=(b, nkv, group * n_q),
        in_specs=[
            pl.BlockSpec((1, 1, dh, tq), lambda bi, g, i: (bi, g * group + i // n_q, 0, i % n_q)),
            pl.BlockSpec((1, 1, s, dh), lambda bi, g, i: (bi, g, 0, 0)),
            pl.BlockSpec((1, 1, n_kv, dh, tk), lambda bi, g, i: (bi, g, 0, 0, 0)),
        ],
        out_specs=pl.BlockSpec((1, 1, tq, dh), lambda bi, g, i: (bi, g * group + i // n_q, i % n_q, 0)),
        out_shape=jax.ShapeDtypeStruct((b, nh, s, dh), BF16),
        scratch_shapes=[stat, stat, pltpu.VMEM((dh, tq), F32)] if running_max else [],
        compiler_params=_compiler_params(3),
        name="flash_attention_online" if running_max else "flash_attention",
    )(qt, k, vt)


def _out_pool_kernel(op_ref, o_ref, on_ref, sgp_ref, sg_ref, sgn_ref, xp_ref, x_ref, xn_ref,
                     mod0_ref, mod1_ref, wao_ref, ng_ref, win_ref, wg_ref, sc_ref, wout_ref, y_ref,
                     *, tm, seq):
    blk = HALO_BLOCK
    rows = tm + 2 * blk
    i = pl.program_id(1)
    n_tiles = pl.num_programs(1)

    def with_halo(prev, main, nxt):
        return jnp.concatenate([prev, main, nxt], axis=0)

    heads = lambda ref: jnp.concatenate([ref[0, hh] for hh in range(N_HEADS)], axis=1)
    o = with_halo(heads(op_ref), heads(o_ref), heads(on_ref))
    sg = with_halo(sgp_ref[0], sg_ref[0], sgn_ref[0])
    x0 = with_halo(xp_ref[0], x_ref[0], xn_ref[0])
    gated = o * sg
    halves = ((0, blk + tm // 2), (blk + tm // 2, rows))
    x1_parts = [x0[lo:hi] + mod0_ref[2, 0] * jnp.dot(gated[lo:hi], wao_ref[...], preferred_element_type=F32)
                for lo, hi in halves]
    y_parts = [jnp.dot(_modulated_rmsnorm(part, ng_ref[...], mod1_ref[1, 0], mod1_ref[0, 0]).astype(BF16),
                       win_ref[...], preferred_element_type=F32) for part in x1_parts]
    x = jnp.concatenate(x1_parts, axis=0)[blk:blk + tm]
    y = jnp.concatenate(y_parts, axis=0)
    u = y[:, :POOL_WIDTH]
    u = jnp.concatenate([jnp.where(i > 0, u[:blk], 0.0),
                         u[blk:blk + tm],
                         jnp.where(i < n_tiles - 1, u[blk + tm:], 0.0)], axis=0)
    gate = y[blk:blk + tm, POOL_WIDTH:]

    edge = V7X_SUBLANES
    t_first = lax.broadcasted_iota(jnp.int32, (edge, 1), 0) + i * tm
    t_last = t_first + (tm - edge)

    def clipped_window_fix(mean_w, w):
        def rescale(t):
            cnt = jnp.minimum(t + w // 2, seq) - jnp.maximum(t - w // 2, 0)
            return float(w) / cnt.astype(F32)
        return jnp.concatenate([mean_w[:edge] * rescale(t_first), mean_w[edge:tm - edge],
                                mean_w[tm - edge:] * rescale(t_last)], axis=0)

    ahead = lambda a, d: pltpu.roll(a, rows - d, 0)
    behind = lambda a, d: pltpu.roll(a, d, 0)

    def gated_out(j, mixed_j):
        cols = slice(j * POOL_GROUP_WIDTH, (j + 1) * POOL_GROUP_WIDTH)
        z = (mixed_j * sc_ref[:, cols] * _silu(gate[:, cols])).astype(BF16)
        return jnp.dot(z, wout_ref[cols, :], preferred_element_type=F32)

    out = None
    pending = None
    for j, w in enumerate(POOL_WINDOWS):
        ug = u[:, j * POOL_GROUP_WIDTH:(j + 1) * POOL_GROUP_WIDTH]
        run, span = ug, 1
        while 2 * span < w:
            run, span = run + ahead(run, span), 2 * span
        win = behind(run, span) + run
        pooled = clipped_window_fix(win[blk:blk + tm] * (1.0 / w), w)
        mix = (pooled - ug[blk:blk + tm]).astype(BF16)
        mixed_j = jnp.dot(mix, wg_ref[j], preferred_element_type=F32)
        if pending is not None:
            part = gated_out(*pending)
            out = part if out is None else out + part
        pending = (j, mixed_j)
    out = out + gated_out(*pending)
    y_ref[0] = x + mod1_ref[2, 0] * out


def _out_proj_and_pool_layer(o, sg, x, mod0, mod1, w_attn_out, norm_g, w_in, w_group, scale, w_out):
    b, s, _ = x.shape
    tm = min(ROW_TILE, s)
    blk = HALO_BLOCK
    per = tm // blk
    last = s // blk - 1
    prev = lambda i: jnp.maximum(i * per - 1, 0)
    nxt = lambda i: jnp.minimum((i + 1) * per, last)
    mod_spec = pl.BlockSpec((3, 1, 1, D_MODEL), lambda bi, i: (0, bi, 0, 0))

    def rows_specs(width):
        return [pl.BlockSpec((1, blk, width), lambda bi, i: (bi, prev(i), 0)),
                pl.BlockSpec((1, tm, width), lambda bi, i: (bi, i, 0)),
                pl.BlockSpec((1, blk, width), lambda bi, i: (bi, nxt(i), 0))]

    kern = functools.partial(_out_pool_kernel, tm=tm, seq=s)
    return pl.pallas_call(
        kern,
        grid=(b, s // tm),
        in_specs=[
            pl.BlockSpec((1, N_HEADS, blk, HEAD_DIM), lambda bi, i: (bi, 0, prev(i), 0)),
            pl.BlockSpec((1, N_HEADS, tm, HEAD_DIM), lambda bi, i: (bi, 0, i, 0)),
            pl.BlockSpec((1, N_HEADS, blk, HEAD_DIM), lambda bi, i: (bi, 0, nxt(i), 0)),
            *rows_specs(Q_DIM),
            *rows_specs(D_MODEL),
            mod_spec,
            mod_spec,
            _resident_spec((Q_DIM, D_MODEL)),
            _resident_spec((1, D_MODEL)),
            _resident_spec((D_MODEL, 2 * POOL_WIDTH)),
            _resident_spec((N_POOL_GROUPS, POOL_GROUP_WIDTH, POOL_GROUP_WIDTH)),
            _resident_spec((1, POOL_WIDTH)),
            _resident_spec((POOL_WIDTH, D_MODEL)),
        ],
        out_specs=pl.BlockSpec((1, tm, D_MODEL), lambda bi, i: (bi, i, 0)),
        out_shape=jax.ShapeDtypeStruct((b, s, D_MODEL), F32),
        compiler_params=_compiler_params(2),
        name="out_proj_pool_layer",
    )(o, o, o, sg, sg, sg, x, x, x, mod0, mod1, w_attn_out, norm_g, w_in, w_group, scale, w_out)


def _rope_tables(s):
    rows = s // GRID_W
    inv = ROPE_THETA ** (-jnp.arange(0, AXIS_DIM, 2, dtype=F32) / AXIS_DIM)
    n_freq = inv.shape[0]
    row_ang = jnp.arange(rows, dtype=F32)[:, None] * inv
    col_ang = jnp.arange(GRID_W, dtype=F32)[:, None] * inv

    def on_grid(fn):
        r = jnp.broadcast_to(fn(row_ang)[:, None, :], (rows, GRID_W, n_freq))
        c = jnp.broadcast_to(fn(col_ang)[None, :, :], (rows, GRID_W, n_freq))
        return jnp.concatenate([r, c], axis=-1).reshape(s, 2 * n_freq)

    c, sn = on_grid(jnp.cos), on_grid(jnp.sin)
    return jnp.concatenate([c, c], axis=-1), jnp.concatenate([-sn, sn], axis=-1)


def _gained_rope_tables(cos_t, sin_t, gain):
    half = HEAD_DIM // 2
    partner_gain = jnp.concatenate([gain[half:], gain[:half]])
    return cos_t * gain, sin_t * partner_gain


def _logit_bound(q_gain, k_gain):
    return (HEAD_DIM ** 0.5 * 1.01) * jnp.max(jnp.abs(q_gain)) * jnp.max(jnp.abs(k_gain))


def _deinterleave(a, n_heads):
    lead = a.shape[:-1]
    a = a.reshape(lead + (n_heads, HEAD_DIM // 2, 2))
    a = jnp.concatenate([a[..., 0], a[..., 1]], axis=-1)
    return a.reshape(lead + (n_heads * HEAD_DIM,))


def _trunk(x, mod, params):
    (norm_g, w_t, w_kg, rope_q, rope_k, small_logits, attn_w_out,
     pool_w_in, pool_w_group, pool_scale, pool_w_out) = params
    qt, k, vt, sg = _attn_in_proj(x, mod[0], norm_g[0:1], w_t, w_kg, rope_q, rope_k)
    o = lax.cond(small_logits,
                 functools.partial(_flash_attention, running_max=False),
                 functools.partial(_flash_attention, running_max=True),
                 qt, k, vt)
    return _out_proj_and_pool_layer(o, sg, x, mod[0], mod[1], attn_w_out, norm_g[1:2],
                                    pool_w_in, pool_w_group, pool_scale, pool_w_out)


def kernel(x_prompt, x_sample, c_prompt, c_sample, norm_g, ada_w, ada_b, attn_w_in, attn_q_norm,
           attn_k_norm, attn_w_out, pool_w_in, pool_w_group, pool_scale, pool_w_out):
    assert DEPTH == 2 and attn_w_in.shape[0] == 1 and pool_w_in.shape[0] == 1
    bp, bs = x_prompt.shape[0], x_sample.shape[0]
    s = x_prompt.shape[1]
    assert x_sample.shape[1] == s and s % ROW_TILE == 0 and s % KV_TILE == 0

    c_all = jnp.concatenate([c_prompt, c_sample], axis=0)
    pad = (-c_all.shape[0]) % V7X_SUBLANES
    c_pad = jnp.pad(c_all, ((0, pad), (0, 0)))
    mod = _modulation(c_pad, ada_w, ada_b)
    mod = mod.reshape(DEPTH, 3, c_pad.shape[0], 1, D_MODEL)

    w_in = attn_w_in[0]
    w_q = _deinterleave(w_in[:, :Q_DIM], N_HEADS)
    w_k = _deinterleave(w_in[:, Q_DIM:Q_DIM + KV_DIM], N_KV_HEADS)
    w_v = w_in[:, Q_DIM + KV_DIM:Q_DIM + 2 * KV_DIM]
    w_t = jnp.concatenate([w_q, w_v], axis=1).T.astype(BF16)
    w_kg = jnp.concatenate([w_k, w_in[:, Q_DIM + 2 * KV_DIM:]], axis=1).astype(BF16)
    cos_t, sin_t = _rope_tables(s)
    gq = _deinterleave(attn_q_norm[0], 1) * (HEAD_DIM ** -0.5 * LOG2_E)
    rope_q = tuple(t.T for t in _gained_rope_tables(cos_t, sin_t, gq))
    rope_k = _gained_rope_tables(cos_t, sin_t, _deinterleave(attn_k_norm[0], 1))
    small_logits = _logit_bound(attn_q_norm[0], attn_k_norm[0]) <= MAX_UNSHIFTED_LOGIT
    params = (norm_g, w_t, w_kg, rope_q, rope_k, small_logits, attn_w_out[0].astype(BF16),
              pool_w_in[0].astype(BF16), pool_w_group[0].astype(BF16),
              pool_scale[0].reshape(1, POOL_WIDTH), pool_w_out[0].astype(BF16))

    y_prompt = _trunk(x_prompt, mod[:, :, :bp], params)
    y_sample = _trunk(x_sample, mod[:, :, bp:bp + bs], params)
    return (y_prompt, y_sample)
```

```python
import functools

import jax
import jax.numpy as jnp
from jax import lax
from jax.experimental import pallas as pl
from jax.experimental.pallas import tpu as pltpu

D_MODEL = 1024
DEPTH = 2
N_HEADS = 8
N_KV_HEADS = 2
HEAD_DIM = 128
Q_DIM = N_HEADS * HEAD_DIM
KV_DIM = N_KV_HEADS * HEAD_DIM
AXIS_DIM = HEAD_DIM // 2
ROPE_THETA = 10000.0
GRID_W = 64
POOL_WINDOWS = (2, 4, 8, 16)
N_POOL_GROUPS = len(POOL_WINDOWS)
POOL_WIDTH = D_MODEL
POOL_GROUP_WIDTH = POOL_WIDTH // N_POOL_GROUPS
RMS_EPS = 1e-6
LOG2_E = 1.4426950408889634
MAX_UNSHIFTED_LOGIT = 40.0

V7X_SUBLANES = 8
V7X_LANES = 128
V7X_VMEM_BYTES = 64 * 1024 * 1024
VMEM_LIMIT_BYTES = (V7X_VMEM_BYTES * 7) // 8

ROW_TILE = 1024
IN_PROJ_SLAB_ROWS = 256
IN_PROJ_T_SLAB_ROWS = 4 * HEAD_DIM
Q_TILE = 4096
KV_TILE = 512
POOL_HALO = max(POOL_WINDOWS) // 2
HALO_BLOCK = 16
assert POOL_HALO <= V7X_SUBLANES and 2 * POOL_HALO <= HALO_BLOCK

BF16 = jnp.bfloat16
F32 = jnp.float32


def _compiler_params(n_grid_axes):
    return pltpu.CompilerParams(
        dimension_semantics=("arbitrary",) * n_grid_axes,
        vmem_limit_bytes=VMEM_LIMIT_BYTES,
    )


def _resident_spec(shape):
    return pl.BlockSpec(shape, lambda *_: (0,) * len(shape), pipeline_mode=pl.Buffered(1))


def _silu(x):
    h = 0.5 * x
    return h + h * jnp.tanh(h)


def _modulated_rmsnorm(x, norm_g, scl, shift):
    gain = norm_g * (1.0 + scl)
    ms = jnp.mean(x * x, axis=-1, keepdims=True)
    return (x * lax.rsqrt(ms + RMS_EPS)) * gain + shift


def _mod_kernel(c_ref, w_ref, b_ref, o_ref):
    cs = _silu(c_ref[...])
    w = w_ref[0]
    cs_hi = cs.astype(BF16)
    cs_lo = (cs - cs_hi.astype(F32)).astype(BF16)
    w_hi = w.astype(BF16)
    w_lo = (w - w_hi.astype(F32)).astype(BF16)
    dot = functools.partial(jnp.dot, preferred_element_type=F32)
    o_ref[0, 0] = dot(cs_hi, w_hi) + (dot(cs_lo, w_hi) + dot(cs_hi, w_lo)) + b_ref[0, 0]


def _modulation(c_pad, ada_w, ada_b):
    bp = c_pad.shape[0]
    return pl.pallas_call(
        _mod_kernel,
        grid=(DEPTH, 3),
        in_specs=[
            pl.BlockSpec((bp, D_MODEL), lambda i, j: (0, 0)),
            pl.BlockSpec((1, D_MODEL, D_MODEL), lambda i, j: (i, 0, j)),
            pl.BlockSpec((1, 1, 1, D_MODEL), lambda i, j: (i, j, 0, 0)),
        ],
        out_specs=pl.BlockSpec((1, 1, bp, D_MODEL), lambda i, j: (i, j, 0, 0)),
        out_shape=jax.ShapeDtypeStruct((DEPTH, 3, bp, D_MODEL), F32),
        compiler_params=_compiler_params(2),
        name="adaln_modulation",
    )(c_pad, ada_w, ada_b.reshape(DEPTH, 3, 1, D_MODEL))


def _attn_in_kernel(x_ref, mod_ref, ng_ref, wt_ref, w_ref, qa_ref, qb_ref, ka_ref, kb_ref,
                    qt_ref, k_ref, vt_ref, sg_ref):
    tm = x_ref.shape[1]
    half = HEAD_DIM // 2
    slab = IN_PROJ_SLAB_ROWS
    hs, ys = [], []
    for r in range(tm // slab):
        rows = pl.ds(r * slab, slab)
        hs.append(_modulated_rmsnorm(x_ref[0, rows], ng_ref[...], mod_ref[1, 0], mod_ref[0, 0]).astype(BF16))
        ys.append(jnp.dot(hs[-1], w_ref[...], preferred_element_type=F32))
    for r, y in enumerate(ys):
        rows = pl.ds(r * slab, slab)
        for j in range(N_KV_HEADS):
            t = y[:, j * HEAD_DIM:(j + 1) * HEAD_DIM]
            rs = lax.rsqrt(jnp.mean(t * t, axis=1, keepdims=True) + RMS_EPS)
            k_ref[0, j, rows] = ((t * ka_ref[rows] + pltpu.roll(t, half, 1) * kb_ref[rows]) * rs).astype(BF16)
        sg_ref[0, rows] = _silu(y[:, KV_DIM:]).astype(BF16)

    h_t = jnp.concatenate(hs, axis=0).T
    qa = qa_ref[...]
    qb = qb_ref[...]
    n_t_rows = Q_DIM + KV_DIM

    def project_t(c):
        lo = c * IN_PROJ_T_SLAB_ROWS
        return jnp.dot(wt_ref[lo:min(lo + IN_PROJ_T_SLAB_ROWS, n_t_rows), :], h_t, preferred_element_type=F32)

    def finish_t(c, yt):
        for i in range(yt.shape[0] // HEAD_DIM):
            head = c * (IN_PROJ_T_SLAB_ROWS // HEAD_DIM) + i
            t = yt[i * HEAD_DIM:(i + 1) * HEAD_DIM]
            if head < N_HEADS:
                rs = lax.rsqrt(jnp.mean(t * t, axis=0, keepdims=True) + RMS_EPS)
                partner = jnp.concatenate([t[half:], t[:half]], axis=0)
                qt_ref[0, head] = ((t * qa + partner * qb) * rs).astype(BF16)
            else:
                for cc in range(tm // KV_TILE):
                    vt_ref[0, head - N_HEADS, cc] = t[:, cc * KV_TILE:(cc + 1) * KV_TILE].astype(BF16)

    n_t_slabs = pl.cdiv(n_t_rows, IN_PROJ_T_SLAB_ROWS)
    yt_next = project_t(0)
    for c in range(n_t_slabs):
        yt = yt_next
        if c + 1 < n_t_slabs:
            yt_next = project_t(c + 1)
        finish_t(c, yt)


def _attn_in_proj(x, mod, norm_g, w_t, w_kg, rope_q, rope_k):
    b, s, _ = x.shape
    tm = min(ROW_TILE, s)
    n_chunks = tm // KV_TILE
    return pl.pallas_call(
        _attn_in_kernel,
        grid=(b, s // tm),
        in_specs=[
            pl.BlockSpec((1, tm, D_MODEL), lambda bi, i: (bi, i, 0)),
            pl.BlockSpec((3, 1, 1, D_MODEL), lambda bi, i: (0, bi, 0, 0)),
            _resident_spec((1, D_MODEL)),
            _resident_spec((Q_DIM + KV_DIM, D_MODEL)),
            _resident_spec((D_MODEL, KV_DIM + Q_DIM)),
            pl.BlockSpec((HEAD_DIM, tm), lambda bi, i: (0, i)),
            pl.BlockSpec((HEAD_DIM, tm), lambda bi, i: (0, i)),
            pl.BlockSpec((tm, HEAD_DIM), lambda bi, i: (i, 0)),
            pl.BlockSpec((tm, HEAD_DIM), lambda bi, i: (i, 0)),
        ],
        out_specs=[
            pl.BlockSpec((1, N_HEADS, HEAD_DIM, tm), lambda bi, i: (bi, 0, 0, i)),
            pl.BlockSpec((1, N_KV_HEADS, tm, HEAD_DIM), lambda bi, i: (bi, 0, i, 0)),
            pl.BlockSpec((1, N_KV_HEADS, n_chunks, HEAD_DIM, KV_TILE), lambda bi, i: (bi, 0, i, 0, 0)),
            pl.BlockSpec((1, tm, Q_DIM), lambda bi, i: (bi, i, 0)),
        ],
        out_shape=[
            jax.ShapeDtypeStruct((b, N_HEADS, HEAD_DIM, s), BF16),
            jax.ShapeDtypeStruct((b, N_KV_HEADS, s, HEAD_DIM), BF16),
            jax.ShapeDtypeStruct((b, N_KV_HEADS, s // KV_TILE, HEAD_DIM, KV_TILE), BF16),
            jax.ShapeDtypeStruct((b, s, Q_DIM), BF16),
        ],
        compiler_params=_compiler_params(2),
        name="attn_in_proj",
    )(x, mod, norm_g, w_t, w_kg, *rope_q, *rope_k)


def _flash_kernel(small_ref, qt_ref, k_ref, vt_ref, o_ref, m_sc, l_sc, acc_sc, *, n_kv, tk):
    q_t = qt_ref[0, 0]
    tq = q_t.shape[1]

    def scores_t(j):
        return jnp.dot(k_ref[0, 0, pl.ds(j * tk, tk), :], q_t, preferred_element_type=F32)

    @pl.when(small_ref[0] == 0)
    def _online_softmax():
        m_sc[...] = jnp.full(m_sc.shape, -jnp.inf, F32)
        l_sc[...] = jnp.zeros(l_sc.shape, F32)
        acc_sc[...] = jnp.zeros(acc_sc.shape, F32)

        def body(j, carry):
            s_t = scores_t(j)
            m_prev = m_sc[...]
            m_next = jnp.maximum(m_prev, jnp.max(s_t, axis=0, keepdims=True))
            alpha = jnp.exp2(m_prev - m_next)
            p_t = jnp.exp2(s_t - m_next)
            l_sc[...] = alpha * l_sc[...] + jnp.sum(p_t, axis=0, keepdims=True)
            acc_sc[...] = acc_sc[...] * alpha + jnp.dot(
                vt_ref[0, 0, j], p_t.astype(BF16), preferred_element_type=F32)
            m_sc[...] = m_next
            return carry

        lax.fori_loop(0, n_kv, body, 0)
        o_ref[0, 0] = (acc_sc[...] / l_sc[...]).T.astype(BF16)

    @pl.when(small_ref[0] != 0)
    def _unshifted_softmax():
        acc = jnp.zeros((HEAD_DIM, tq), F32)
        l_part = jnp.zeros((V7X_SUBLANES, tq), F32)
        s_next = scores_t(0)
        for j in range(n_kv):
            s_t = s_next
            if j + 1 < n_kv:
                s_next = scores_t(j + 1)
            p_t = jnp.exp2(s_t)
            l_part = l_part + jnp.sum(p_t.reshape(tk // V7X_SUBLANES, V7X_SUBLANES, tq), axis=0)
            acc = acc + jnp.dot(vt_ref[0, 0, j], p_t.astype(BF16), preferred_element_type=F32)
        o_ref[0, 0] = (acc / jnp.sum(l_part, axis=0, keepdims=True)).T.astype(BF16)


def _flash_attention(small_logits, qt, k, vt):
    b, nh, dh, s = qt.shape
    nkv, n_kv, tk = vt.shape[1], vt.shape[2], vt.shape[4]
    group = nh // nkv
    tq = min(Q_TILE, s)
    assert s % tq == 0
    n_q = s // tq
    kern = functools.partial(_flash_kernel, n_kv=n_kv, tk=tk)
    stat = pltpu.VMEM((1, tq), F32)
    grid_spec = pltpu.PrefetchScalarGridSpec(
        num_scalar_prefetch=1,
        grid=(b, nkv, group * n_q),
        in_specs=[
            pl.BlockSpec((1, 1, dh, tq), lambda bi, g, i, _: (bi, g * group + i // n_q, 0, i % n_q)),
            pl.BlockSpec((1, 1, s, dh), lambda bi, g, i, _: (bi, g, 0, 0)),
            pl.BlockSpec((1, 1, n_kv, dh, tk), lambda bi, g, i, _: (bi, g, 0, 0, 0)),
        ],
        out_specs=pl.BlockSpec((1, 1, tq, dh), lambda bi, g, i, _: (bi, g * group + i // n_q, i % n_q, 0)),
        scratch_shapes=[stat, stat, pltpu.VMEM((dh, tq), F32)],
    )
    return pl.pallas_call(
        kern,
        grid_spec=grid_spec,
        out_shape=jax.ShapeDtypeStruct((b, nh, s, dh), BF16),
        compiler_params=_compiler_params(3),
        name="flash_attention",
    )(small_logits, qt, k, vt)


def _out_pool_kernel(op_ref, o_ref, on_ref, sgp_ref, sg_ref, sgn_ref, xp_ref, x_ref, xn_ref,
                     mod0_ref, mod1_ref, wao_ref, ng_ref, win_ref, wg_ref, sc_ref, wout_ref, y_ref,
                     *, tm, seq):
    blk = HALO_BLOCK
    rows = tm + 2 * blk
    i = pl.program_id(1)
    n_tiles = pl.num_programs(1)

    def with_halo(prev, main, nxt):
        return jnp.concatenate([prev, main, nxt], axis=0)

    heads = lambda ref: jnp.concatenate([ref[0, hh] for hh in range(N_HEADS)], axis=1)
    o = with_halo(heads(op_ref), heads(o_ref), heads(on_ref))
    sg = with_halo(sgp_ref[0], sg_ref[0], sgn_ref[0])
    x0 = with_halo(xp_ref[0], x_ref[0], xn_ref[0])
    gated = o * sg
    halves = ((0, blk + tm // 2), (blk + tm // 2, rows))
    x1_parts = [x0[lo:hi] + mod0_ref[2, 0] * jnp.dot(gated[lo:hi], wao_ref[...], preferred_element_type=F32)
                for lo, hi in halves]
    y_parts = [jnp.dot(_modulated_rmsnorm(part, ng_ref[...], mod1_ref[1, 0], mod1_ref[0, 0]).astype(BF16),
                       win_ref[...], preferred_element_type=F32) for part in x1_parts]
    x = jnp.concatenate(x1_parts, axis=0)[blk:blk + tm]
    y = jnp.concatenate(y_parts, axis=0)
    u = y[:, :POOL_WIDTH]
    u = jnp.concatenate([jnp.where(i > 0, u[:blk], 0.0),
                         u[blk:blk + tm],
                         jnp.where(i < n_tiles - 1, u[blk + tm:], 0.0)], axis=0)
    gate = y[blk:blk + tm, POOL_WIDTH:]

    edge = V7X_SUBLANES
    t_first = lax.broadcasted_iota(jnp.int32, (edge, 1), 0) + i * tm
    t_last = t_first + (tm - edge)

    def clipped_window_fix(mean_w, w):
        def rescale(t):
            cnt = jnp.minimum(t + w // 2, seq) - jnp.maximum(t - w // 2, 0)
            return float(w) / cnt.astype(F32)
        return jnp.concatenate([mean_w[:edge] * rescale(t_first), mean_w[edge:tm - edge],
                                mean_w[tm - edge:] * rescale(t_last)], axis=0)

    ahead = lambda a, d: pltpu.roll(a, rows - d, 0)
    behind = lambda a, d: pltpu.roll(a, d, 0)

    def gated_out(j, mixed_j):
        cols = slice(j * POOL_GROUP_WIDTH, (j + 1) * POOL_GROUP_WIDTH)
        z = (mixed_j * sc_ref[:, cols] * _silu(gate[:, cols])).astype(BF16)
        return jnp.dot(z, wout_ref[cols, :], preferred_element_type=F32)

    out = None
    pending = None
    for j, w in enumerate(POOL_WINDOWS):
        ug = u[:, j * POOL_GROUP_WIDTH:(j + 1) * POOL_GROUP_WIDTH]
        run, span = ug, 1
        while 2 * span < w:
            run, span = run + ahead(run, span), 2 * span
        win = behind(run, span) + run
        pooled = clipped_window_fix(win[blk:blk + tm] * (1.0 / w), w)
        mix = (pooled - ug[blk:blk + tm]).astype(BF16)
        mixed_j = jnp.dot(mix, wg_ref[j], preferred_element_type=F32)
        if pending is not None:
            part = gated_out(*pending)
            out = part if out is None else out + part
        pending = (j, mixed_j)
    out = out + gated_out(*pending)
    y_ref[0] = x + mod1_ref[2, 0] * out


def _out_proj_and_pool_layer(o, sg, x, mod0, mod1, w_attn_out, norm_g, w_in, w_group, scale, w_out):
    b, s, _ = x.shape
    tm = min(ROW_TILE, s)
    blk = HALO_BLOCK
    per = tm // blk
    last = s // blk - 1
    prev = lambda i: jnp.maximum(i * per - 1, 0)
    nxt = lambda i: jnp.minimum((i + 1) * per, last)
    mod_spec = pl.BlockSpec((3, 1, 1, D_MODEL), lambda bi, i: (0, bi, 0, 0))

    def rows_specs(width):
        return [pl.BlockSpec((1, blk, width), lambda bi, i: (bi, prev(i), 0)),
                pl.BlockSpec((1, tm, width), lambda bi, i: (bi, i, 0)),
                pl.BlockSpec((1, blk, width), lambda bi, i: (bi, nxt(i), 0))]

    kern = functools.partial(_out_pool_kernel, tm=tm, seq=s)
    return pl.pallas_call(
        kern,
        grid=(b, s // tm),
        in_specs=[
            pl.BlockSpec((1, N_HEADS, blk, HEAD_DIM), lambda bi, i: (bi, 0, prev(i), 0)),
            pl.BlockSpec((1, N_HEADS, tm, HEAD_DIM), lambda bi, i: (bi, 0, i, 0)),
            pl.BlockSpec((1, N_HEADS, blk, HEAD_DIM), lambda bi, i: (bi, 0, nxt(i), 0)),
            *rows_specs(Q_DIM),
            *rows_specs(D_MODEL),
            mod_spec,
            mod_spec,
            _resident_spec((Q_DIM, D_MODEL)),
            _resident_spec((1, D_MODEL)),
            _resident_spec((D_MODEL, 2 * POOL_WIDTH)),
            _resident_spec((N_POOL_GROUPS, POOL_GROUP_WIDTH, POOL_GROUP_WIDTH)),
            _resident_spec((1, POOL_WIDTH)),
            _resident_spec((POOL_WIDTH, D_MODEL)),
        ],
        out_specs=pl.BlockSpec((1, tm, D_MODEL), lambda bi, i: (bi, i, 0)),
        out_shape=jax.ShapeDtypeStruct((b, s, D_MODEL), F32),
        compiler_params=_compiler_params(2),
        name="out_proj_pool_layer",
    )(o, o, o, sg, sg, sg, x, x, x, mod0, mod1, w_attn_out, norm_g, w_in, w_group, scale, w_out)


def _rope_tables(s):
    rows = s // GRID_W
    inv = ROPE_THETA ** (-jnp.arange(0, AXIS_DIM, 2, dtype=F32) / AXIS_DIM)
    n_freq = inv.shape[0]
    row_ang = jnp.arange(rows, dtype=F32)[:, None] * inv
    col_ang = jnp.arange(GRID_W, dtype=F32)[:, None] * inv

    def on_grid(fn):
        r = jnp.broadcast_to(fn(row_ang)[:, None, :], (rows, GRID_W, n_freq))
        c = jnp.broadcast_to(fn(col_ang)[None, :, :], (rows, GRID_W, n_freq))
        return jnp.concatenate([r, c], axis=-1).reshape(s, 2 * n_freq)

    c, sn = on_grid(jnp.cos), on_grid(jnp.sin)
    return jnp.concatenate([c, c], axis=-1), jnp.concatenate([-sn, sn], axis=-1)


def _gained_rope_tables(cos_t, sin_t, gain):
    half = HEAD_DIM // 2
    partner_gain = jnp.concatenate([gain[half:], gain[:half]])
    return cos_t * gain, sin_t * partner_gain


def _logit_bound(q_gain, k_gain):
    return (HEAD_DIM ** 0.5 * 1.01) * jnp.max(jnp.abs(q_gain)) * jnp.max(jnp.abs(k_gain))


def _deinterleave(a, n_heads):
    lead = a.shape[:-1]
    a = a.reshape(lead + (n_heads, HEAD_DIM // 2, 2))
    a = jnp.concatenate([a[..., 0], a[..., 1]], axis=-1)
    return a.reshape(lead + (n_heads * HEAD_DIM,))


def _trunk(x, mod, params):
    (norm_g, w_t, w_kg, rope_q, rope_k, small_logits, attn_w_out,
     pool_w_in, pool_w_group, pool_scale, pool_w_out) = params
    qt, k, vt, sg = _attn_in_proj(x, mod[0], norm_g[0:1], w_t, w_kg, rope_q, rope_k)
    o = _flash_attention(small_logits, qt, k, vt)
    return _out_proj_and_pool_layer(o, sg, x, mod[0], mod[1], attn_w_out, norm_g[1:2],
                                    pool_w_in, pool_w_group, pool_scale, pool_w_out)


def kernel(x_prompt, x_sample, c_prompt, c_sample, norm_g, ada_w, ada_b, attn_w_in, attn_q_norm,
           attn_k_norm, attn_w_out, pool_w_in, pool_w_group, pool_scale, pool_w_out):
    assert DEPTH == 2 and attn_w_in.shape[0] == 1 and pool_w_in.shape[0] == 1
    bp, bs = x_prompt.shape[0], x_sample.shape[0]
    s = x_prompt.shape[1]
    assert x_sample.shape[1] == s and s % ROW_TILE == 0 and s % KV_TILE == 0

    c_all = jnp.concatenate([c_prompt, c_sample], axis=0)
    pad = (-c_all.shape[0]) % V7X_SUBLANES
    c_pad = jnp.pad(c_all, ((0, pad), (0, 0)))
    mod = _modulation(c_pad, ada_w, ada_b)
    mod = mod.reshape(DEPTH, 3, c_pad.shape[0], 1, D_MODEL)

    w_in = attn_w_in[0]
    w_q = _deinterleave(w_in[:, :Q_DIM], N_HEADS)
    w_k = _deinterleave(w_in[:, Q_DIM:Q_DIM + KV_DIM], N_KV_HEADS)
    w_v = w_in[:, Q_DIM + KV_DIM:Q_DIM + 2 * KV_DIM]
    w_t = jnp.concatenate([w_q, w_v], axis=1).T.astype(BF16)
    w_kg = jnp.concatenate([w_k, w_in[:, Q_DIM + 2 * KV_DIM:]], axis=1).astype(BF16)
    cos_t, sin_t = _rope_tables(s)
    gq = _deinterleave(attn_q_norm[0], 1) * (HEAD_DIM ** -0.5 * LOG2_E)
    rope_q = tuple(t.T for t in _gained_rope_tables(cos_t, sin_t, gq))
    rope_k = _gained_rope_tables(cos_t, sin_t, _deinterleave(attn_k_norm[0], 1))
    small_logits = (_logit_bound(attn_q_norm[0], attn_k_norm[0]) <= MAX_UNSHIFTED_LOGIT).astype(jnp.int32).reshape(1)
    params = (norm_g, w_t, w_kg, rope_q, rope_k, small_logits, attn_w_out[0].astype(BF16),
              pool_w_in[0].astype(BF16), pool_w_group[0].astype(BF16),
              pool_scale[0].reshape(1, POOL_WIDTH), pool_w_out[0].astype(BF16))

    y_prompt = _trunk(x_prompt, mod[:, :, :bp], params)
    y_sample = _trunk(x_sample, mod[:, :, bp:bp + bs], params)
    return (y_prompt, y_sample)
```

```python
import functools

import jax
import jax.numpy as jnp
from jax import lax
from jax.experimental import pallas as pl
from jax.experimental.pallas import tpu as pltpu

D_MODEL = 1024
DEPTH = 2
N_HEADS = 8
N_KV_HEADS = 2
HEAD_DIM = 128
Q_DIM = N_HEADS * HEAD_DIM
KV_DIM = N_KV_HEADS * HEAD_DIM
AXIS_DIM = HEAD_DIM // 2
ROPE_THETA = 10000.0
GRID_W = 64
POOL_WINDOWS = (2, 4, 8, 16)
N_POOL_GROUPS = len(POOL_WINDOWS)
POOL_WIDTH = D_MODEL
POOL_GROUP_WIDTH = POOL_WIDTH // N_POOL_GROUPS
RMS_EPS = 1e-6
LOG2_E = 1.4426950408889634
MAX_UNSHIFTED_LOGIT = 40.0

V7X_SUBLANES = 8
V7X_LANES = 128
V7X_VMEM_BYTES = 64 * 1024 * 1024
VMEM_LIMIT_BYTES = (V7X_VMEM_BYTES * 7) // 8

ROW_TILE = 1024
IN_PROJ_SLAB_ROWS = 256
IN_PROJ_T_SLAB_ROWS = 5 * HEAD_DIM
Q_TILE = 4096
KV_TILE = 512
POOL_HALO = max(POOL_WINDOWS) // 2
HALO_BLOCK = 16
assert POOL_HALO <= V7X_SUBLANES and 2 * POOL_HALO <= HALO_BLOCK

BF16 = jnp.bfloat16
F32 = jnp.float32


def _compiler_params(n_grid_axes):
    return pltpu.CompilerParams(
        dimension_semantics=("arbitrary",) * n_grid_axes,
        vmem_limit_bytes=VMEM_LIMIT_BYTES,
    )


def _resident_spec(shape):
    return pl.BlockSpec(shape, lambda *_: (0,) * len(shape), pipeline_mode=pl.Buffered(1))


def _silu(x):
    h = 0.5 * x
    return h + h * jnp.tanh(h)


def _modulated_rmsnorm(x, norm_g, scl, shift):
    gain = norm_g * (1.0 + scl)
    ms = jnp.mean(x * x, axis=-1, keepdims=True)
    return (x * lax.rsqrt(ms + RMS_EPS)) * gain + shift


def _mod_kernel(c_ref, w_ref, b_ref, o_ref):
    cs = _silu(c_ref[...])
    w = w_ref[0]
    cs_hi = cs.astype(BF16)
    cs_lo = (cs - cs_hi.astype(F32)).astype(BF16)
    w_hi = w.astype(BF16)
    w_lo = (w - w_hi.astype(F32)).astype(BF16)
    dot = functools.partial(jnp.dot, preferred_element_type=F32)
    o_ref[0, 0] = dot(cs_hi, w_hi) + (dot(cs_lo, w_hi) + dot(cs_hi, w_lo)) + b_ref[0, 0]


def _modulation(c_pad, ada_w, ada_b):
    bp = c_pad.shape[0]
    return pl.pallas_call(
        _mod_kernel,
        grid=(DEPTH, 3),
        in_specs=[
            pl.BlockSpec((bp, D_MODEL), lambda i, j: (0, 0)),
            pl.BlockSpec((1, D_MODEL, D_MODEL), lambda i, j: (i, 0, j)),
            pl.BlockSpec((1, 1, 1, D_MODEL), lambda i, j: (i, j, 0, 0)),
        ],
        out_specs=pl.BlockSpec((1, 1, bp, D_MODEL), lambda i, j: (i, j, 0, 0)),
        out_shape=jax.ShapeDtypeStruct((DEPTH, 3, bp, D_MODEL), F32),
        compiler_params=_compiler_params(2),
        name="adaln_modulation",
    )(c_pad, ada_w, ada_b.reshape(DEPTH, 3, 1, D_MODEL))


def _attn_in_kernel(x_ref, mod_ref, ng_ref, wt_ref, w_ref, qa_ref, qb_ref, ka_ref, kb_ref,
                    qt_ref, k_ref, vt_ref, sg_ref):
    tm = x_ref.shape[1]
    half = HEAD_DIM // 2
    slab = IN_PROJ_SLAB_ROWS
    hs, ys = [], []
    for r in range(tm // slab):
        rows = pl.ds(r * slab, slab)
        hs.append(_modulated_rmsnorm(x_ref[0, rows], ng_ref[...], mod_ref[1, 0], mod_ref[0, 0]).astype(BF16))
        ys.append(jnp.dot(hs[-1], w_ref[...], preferred_element_type=F32))
    for r, y in enumerate(ys):
        rows = pl.ds(r * slab, slab)
        for j in range(N_KV_HEADS):
            t = y[:, j * HEAD_DIM:(j + 1) * HEAD_DIM]
            rs = lax.rsqrt(jnp.mean(t * t, axis=1, keepdims=True) + RMS_EPS)
            k_ref[0, j, rows] = ((t * ka_ref[rows] + pltpu.roll(t, half, 1) * kb_ref[rows]) * rs).astype(BF16)
        sg_ref[0, rows] = _silu(y[:, KV_DIM:]).astype(BF16)

    h_t = jnp.concatenate(hs, axis=0).T
    qa = qa_ref[...]
    qb = qb_ref[...]
    n_t_rows = Q_DIM + KV_DIM

    def project_t(c):
        lo = c * IN_PROJ_T_SLAB_ROWS
        return jnp.dot(wt_ref[lo:min(lo + IN_PROJ_T_SLAB_ROWS, n_t_rows), :], h_t, preferred_element_type=F32)

    def finish_t(c, yt):
        for i in range(yt.shape[0] // HEAD_DIM):
            head = c * (IN_PROJ_T_SLAB_ROWS // HEAD_DIM) + i
            t = yt[i * HEAD_DIM:(i + 1) * HEAD_DIM]
            if head < N_HEADS:
                rs = lax.rsqrt(jnp.mean(t * t, axis=0, keepdims=True) + RMS_EPS)
                partner = jnp.concatenate([t[half:], t[:half]], axis=0)
                qt_ref[0, head] = ((t * qa + partner * qb) * rs).astype(BF16)
            else:
                for cc in range(tm // KV_TILE):
                    vt_ref[0, head - N_HEADS, cc] = t[:, cc * KV_TILE:(cc + 1) * KV_TILE].astype(BF16)

    n_t_slabs = pl.cdiv(n_t_rows, IN_PROJ_T_SLAB_ROWS)
    yt_next = project_t(0)
    for c in range(n_t_slabs):
        yt = yt_next
        if c + 1 < n_t_slabs:
            yt_next = project_t(c + 1)
        finish_t(c, yt)


def _attn_in_proj(x, mod, norm_g, w_t, w_kg, rope_q, rope_k):
    b, s, _ = x.shape
    tm = min(ROW_TILE, s)
    n_chunks = tm // KV_TILE
    return pl.pallas_call(
        _attn_in_kernel,
        grid=(b, s // tm),
        in_specs=[
            pl.BlockSpec((1, tm, D_MODEL), lambda bi, i: (bi, i, 0)),
            pl.BlockSpec((3, 1, 1, D_MODEL), lambda bi, i: (0, bi, 0, 0)),
            _resident_spec((1, D_MODEL)),
            _resident_spec((Q_DIM + KV_DIM, D_MODEL)),
            _resident_spec((D_MODEL, KV_DIM + Q_DIM)),
            pl.BlockSpec((HEAD_DIM, tm), lambda bi, i: (0, i)),
            pl.BlockSpec((HEAD_DIM, tm), lambda bi, i: (0, i)),
            pl.BlockSpec((tm, HEAD_DIM), lambda bi, i: (i, 0)),
            pl.BlockSpec((tm, HEAD_DIM), lambda bi, i: (i, 0)),
        ],
        out_specs=[
            pl.BlockSpec((1, N_HEADS, HEAD_DIM, tm), lambda bi, i: (bi, 0, 0, i)),
            pl.BlockSpec((1, N_KV_HEADS, tm, HEAD_DIM), lambda bi, i: (bi, 0, i, 0)),
            pl.BlockSpec((1, N_KV_HEADS, n_chunks, HEAD_DIM, KV_TILE), lambda bi, i: (bi, 0, i, 0, 0)),
            pl.BlockSpec((1, tm, Q_DIM), lambda bi, i: (bi, i, 0)),
        ],
        out_shape=[
            jax.ShapeDtypeStruct((b, N_HEADS, HEAD_DIM, s), BF16),
            jax.ShapeDtypeStruct((b, N_KV_HEADS, s, HEAD_DIM), BF16),
            jax.ShapeDtypeStruct((b, N_KV_HEADS, s // KV_TILE, HEAD_DIM, KV_TILE), BF16),
            jax.ShapeDtypeStruct((b, s, Q_DIM), BF16),
        ],
        compiler_params=_compiler_params(2),
        name="attn_in_proj",
    )(x, mod, norm_g, w_t, w_kg, *rope_q, *rope_k)


def _flash_kernel(qt_ref, k_ref, vt_ref, o_ref, *scratch, n_kv, tk, running_max):
    q_t = qt_ref[0, 0]
    tq = q_t.shape[1]

    def scores_t(j):
        return jnp.dot(k_ref[0, 0, pl.ds(j * tk, tk), :], q_t, preferred_element_type=F32)

    if running_max:
        m_sc, l_sc, acc_sc = scratch
        m_sc[...] = jnp.full(m_sc.shape, -jnp.inf, F32)
        l_sc[...] = jnp.zeros(l_sc.shape, F32)
        acc_sc[...] = jnp.zeros(acc_sc.shape, F32)

        def body(j, carry):
            s_t = scores_t(j)
            m_prev = m_sc[...]
            m_next = jnp.maximum(m_prev, jnp.max(s_t, axis=0, keepdims=True))
            alpha = jnp.exp2(m_prev - m_next)
            p_t = jnp.exp2(s_t - m_next)
            l_sc[...] = alpha * l_sc[...] + jnp.sum(p_t, axis=0, keepdims=True)
            acc_sc[...] = acc_sc[...] * alpha + jnp.dot(
                vt_ref[0, 0, j], p_t.astype(BF16), preferred_element_type=F32)
            m_sc[...] = m_next
            return carry

        lax.fori_loop(0, n_kv, body, 0)
        o_t = acc_sc[...] / l_sc[...]
    else:
        acc = jnp.zeros((HEAD_DIM, tq), F32)
        l_part = jnp.zeros((V7X_SUBLANES, tq), F32)
        s_next = scores_t(0)
        for j in range(n_kv):
            s_t = s_next
            if j + 1 < n_kv:
                s_next = scores_t(j + 1)
            p_t = jnp.exp2(s_t)
            l_part = l_part + jnp.sum(p_t.reshape(tk // V7X_SUBLANES, V7X_SUBLANES, tq), axis=0)
            acc = acc + jnp.dot(vt_ref[0, 0, j], p_t.astype(BF16), preferred_element_type=F32)
        o_t = acc / jnp.sum(l_part, axis=0, keepdims=True)
    o_ref[0, 0] = o_t.T.astype(BF16)


def _flash_attention(qt, k, vt, *, running_max):
    b, nh, dh, s = qt.shape
    nkv, n_kv, tk = vt.shape[1], vt.shape[2], vt.shape[4]
    group = nh // nkv
    tq = min(Q_TILE, s)
    assert s % tq == 0
    n_q = s // tq
    kern = functools.partial(_flash_kernel, n_kv=n_kv, tk=tk, running_max=running_max)
    stat = pltpu.VMEM((1, tq), F32)
    return pl.pallas_call(
        kern,
        grid=(b, nkv, group * n_q),
        in_specs=[
            pl.BlockSpec((1, 1, dh, tq), lambda bi, g, i: (bi, g * group + i // n_q, 0, i % n_q)),
            pl.BlockSpec((1, 1, s, dh), lambda bi, g, i: (bi, g, 0, 0)),
            pl.BlockSpec((1, 1, n_kv, dh, tk), lambda bi, g, i: (bi, g, 0, 0, 0)),
        ],
        out_specs=pl.BlockSpec((1, 1, tq, dh), lambda bi, g, i: (bi, g * group + i // n_q, i % n_q, 0)),
        out_shape=jax.ShapeDtypeStruct((b, nh, s, dh), BF16),
        scratch_shapes=[stat, stat, pltpu.VMEM((dh, tq), F32)] if running_max else [],
        compiler_params=_compiler_params(3),
        name="flash_attention_online" if running_max else "flash_attention",
    )(qt, k, vt)


def _out_pool_kernel(op_ref, o_ref, on_ref, sgp_ref, sg_ref, sgn_ref, xp_ref, x_ref, xn_ref,
                     mod_ref, wao_ref, rowp_ref, win_ref, wg_ref, wout_ref, y_ref,
                     *, tm, seq):
    blk = HALO_BLOCK
    rows = tm + 2 * blk
    i = pl.program_id(1)
    n_tiles = pl.num_programs(1)

    def with_halo(prev, main, nxt):
        return jnp.concatenate([prev, main, nxt], axis=0)

    heads = lambda ref: jnp.concatenate([ref[0, hh] for hh in range(N_HEADS)], axis=1)
    o = with_halo(heads(op_ref), heads(o_ref), heads(on_ref))
    sg = with_halo(sgp_ref[0], sg_ref[0], sgn_ref[0])
    x0 = with_halo(xp_ref[0], x_ref[0], xn_ref[0])
    gated = o * sg
    halves = ((0, blk + tm // 2), (blk + tm // 2, rows))
    x1_parts = [x0[lo:hi] + mod_ref[0, 2, 0] * jnp.dot(gated[lo:hi], wao_ref[...], preferred_element_type=F32)
                for lo, hi in halves]
    y_parts = [jnp.dot(_modulated_rmsnorm(part, rowp_ref[0:1], mod_ref[1, 1, 0], mod_ref[1, 0, 0]).astype(BF16),
                       win_ref[...], preferred_element_type=F32) for part in x1_parts]
    x = jnp.concatenate(x1_parts, axis=0)[blk:blk + tm]
    y = jnp.concatenate(y_parts, axis=0)
    u = y[:, :POOL_WIDTH]
    u = jnp.concatenate([jnp.where(i > 0, u[:blk], 0.0),
                         u[blk:blk + tm],
                         jnp.where(i < n_tiles - 1, u[blk + tm:], 0.0)], axis=0)
    gate = y[blk:blk + tm, POOL_WIDTH:]

    edge = V7X_SUBLANES
    t_first = lax.broadcasted_iota(jnp.int32, (edge, 1), 0) + i * tm
    t_last = t_first + (tm - edge)

    def clipped_window_fix(mean_w, w):
        def rescale(t):
            cnt = jnp.minimum(t + w // 2, seq) - jnp.maximum(t - w // 2, 0)
            return float(w) / cnt.astype(F32)
        return jnp.concatenate([mean_w[:edge] * rescale(t_first), mean_w[edge:tm - edge],
                                mean_w[tm - edge:] * rescale(t_last)], axis=0)

    ahead = lambda a, d: pltpu.roll(a, rows - d, 0)
    behind = lambda a, d: pltpu.roll(a, d, 0)

    def gated_out(j, mixed_j):
        cols = slice(j * POOL_GROUP_WIDTH, (j + 1) * POOL_GROUP_WIDTH)
        z = (mixed_j * rowp_ref[1:2, cols] * _silu(gate[:, cols])).astype(BF16)
        return jnp.dot(z, wout_ref[cols, :], preferred_element_type=F32)

    out = None
    pending = None
    for j, w in enumerate(POOL_WINDOWS):
        ug = u[:, j * POOL_GROUP_WIDTH:(j + 1) * POOL_GROUP_WIDTH]
        run, span = ug, 1
        while 2 * span < w:
            run, span = run + ahead(run, span), 2 * span
        win = behind(run, span) + run
        pooled = clipped_window_fix(win[blk:blk + tm] * (1.0 / w), w)
        mix = (pooled - ug[blk:blk + tm]).astype(BF16)
        mixed_j = jnp.dot(mix, wg_ref[j], preferred_element_type=F32)
        if pending is not None:
            part = gated_out(*pending)
            out = part if out is None else out + part
        pending = (j, mixed_j)
    out = out + gated_out(*pending)
    y_ref[0] = x + mod_ref[1, 2, 0] * out


def _out_proj_and_pool_layer(o, sg, x, mod, w_attn_out, row_params, w_in, w_group, w_out):
    b, s, _ = x.shape
    tm = min(ROW_TILE, s)
    blk = HALO_BLOCK
    per = tm // blk
    last = s // blk - 1
    prev = lambda i: jnp.maximum(i * per - 1, 0)
    nxt = lambda i: jnp.minimum((i + 1) * per, last)
    mod_spec = pl.BlockSpec((DEPTH, 3, 1, 1, D_MODEL), lambda bi, i: (0, 0, bi, 0, 0))

    def rows_specs(width):
        return [pl.BlockSpec((1, blk, width), lambda bi, i: (bi, prev(i), 0)),
                pl.BlockSpec((1, tm, width), lambda bi, i: (bi, i, 0)),
                pl.BlockSpec((1, blk, width), lambda bi, i: (bi, nxt(i), 0))]

    kern = functools.partial(_out_pool_kernel, tm=tm, seq=s)
    return pl.pallas_call(
        kern,
        grid=(b, s // tm),
        in_specs=[
            pl.BlockSpec((1, N_HEADS, blk, HEAD_DIM), lambda bi, i: (bi, 0, prev(i), 0)),
            pl.BlockSpec((1, N_HEADS, tm, HEAD_DIM), lambda bi, i: (bi, 0, i, 0)),
            pl.BlockSpec((1, N_HEADS, blk, HEAD_DIM), lambda bi, i: (bi, 0, nxt(i), 0)),
            *rows_specs(Q_DIM),
            *rows_specs(D_MODEL),
            mod_spec,
            _resident_spec((Q_DIM, D_MODEL)),
            _resident_spec((2, D_MODEL)),
            _resident_spec((D_MODEL, 2 * POOL_WIDTH)),
            _resident_spec((N_POOL_GROUPS, POOL_GROUP_WIDTH, POOL_GROUP_WIDTH)),
            _resident_spec((POOL_WIDTH, D_MODEL)),
        ],
        out_specs=pl.BlockSpec((1, tm, D_MODEL), lambda bi, i: (bi, i, 0)),
        out_shape=jax.ShapeDtypeStruct((b, s, D_MODEL), F32),
        compiler_params=_compiler_params(2),
        name="out_proj_pool_layer",
    )(o, o, o, sg, sg, sg, x, x, x, mod, w_attn_out, row_params, w_in, w_group, w_out)


def _rope_tables(s):
    rows = s // GRID_W
    inv = ROPE_THETA ** (-jnp.arange(0, AXIS_DIM, 2, dtype=F32) / AXIS_DIM)
    n_freq = inv.shape[0]
    row_ang = jnp.arange(rows, dtype=F32)[:, None] * inv
    col_ang = jnp.arange(GRID_W, dtype=F32)[:, None] * inv

    def on_grid(fn):
        r = jnp.broadcast_to(fn(row_ang)[:, None, :], (rows, GRID_W, n_freq))
        c = jnp.broadcast_to(fn(col_ang)[None, :, :], (rows, GRID_W, n_freq))
        return jnp.concatenate([r, c], axis=-1).reshape(s, 2 * n_freq)

    c, sn = on_grid(jnp.cos), on_grid(jnp.sin)
    return jnp.concatenate([c, c], axis=-1), jnp.concatenate([-sn, sn], axis=-1)


def _gained_rope_tables(cos_t, sin_t, gain):
    half = HEAD_DIM // 2
    partner_gain = jnp.concatenate([gain[half:], gain[:half]])
    return cos_t * gain, sin_t * partner_gain


def _logit_bound(q_gain, k_gain):
    return (HEAD_DIM ** 0.5 * 1.01) * jnp.max(jnp.abs(q_gain)) * jnp.max(jnp.abs(k_gain))


def _deinterleave(a, n_heads):
    lead = a.shape[:-1]
    a = a.reshape(lead + (n_heads, HEAD_DIM // 2, 2))
    a = jnp.concatenate([a[..., 0], a[..., 1]], axis=-1)
    return a.reshape(lead + (n_heads * HEAD_DIM,))


def _trunk(x, mod, params):
    (norm_g, w_t, w_kg, rope_q, rope_k, small_logits, attn_w_out,
     pool_w_in, pool_w_group, pool_scale, pool_w_out) = params
    qt, k, vt, sg = _attn_in_proj(x, mod[0], norm_g[0:1], w_t, w_kg, rope_q, rope_k)
    o = lax.cond(small_logits,
                 functools.partial(_flash_attention, running_max=False),
                 functools.partial(_flash_attention, running_max=True),
                 qt, k, vt)
    row_params = jnp.concatenate([norm_g[1:2], pool_scale], axis=0)
    return _out_proj_and_pool_layer(o, sg, x, mod, attn_w_out, row_params, pool_w_in, pool_w_group, pool_w_out)


def kernel(x_prompt, x_sample, c_prompt, c_sample, norm_g, ada_w, ada_b, attn_w_in, attn_q_norm,
           attn_k_norm, attn_w_out, pool_w_in, pool_w_group, pool_scale, pool_w_out):
    assert DEPTH == 2 and attn_w_in.shape[0] == 1 and pool_w_in.shape[0] == 1
    bp, bs = x_prompt.shape[0], x_sample.shape[0]
    s = x_prompt.shape[1]
    assert x_sample.shape[1] == s and s % ROW_TILE == 0 and s % KV_TILE == 0

    c_all = jnp.concatenate([c_prompt, c_sample], axis=0)
    pad = (-c_all.shape[0]) % V7X_SUBLANES
    c_pad = jnp.pad(c_all, ((0, pad), (0, 0)))
    mod = _modulation(c_pad, ada_w, ada_b)
    mod = mod.reshape(DEPTH, 3, c_pad.shape[0], 1, D_MODEL)

    w_in = attn_w_in[0]
    w_q = _deinterleave(w_in[:, :Q_DIM], N_HEADS)
    w_k = _deinterleave(w_in[:, Q_DIM:Q_DIM + KV_DIM], N_KV_HEADS)
    w_v = w_in[:, Q_DIM + KV_DIM:Q_DIM + 2 * KV_DIM]
    w_t = jnp.concatenate([w_q, w_v], axis=1).T.astype(BF16)
    w_kg = jnp.concatenate([w_k, w_in[:, Q_DIM + 2 * KV_DIM:]], axis=1).astype(BF16)
    cos_t, sin_t = _rope_tables(s)
    gq = _deinterleave(attn_q_norm[0], 1) * (HEAD_DIM ** -0.5 * LOG2_E)
    rope_q = tuple(t.T for t in _gained_rope_tables(cos_t, sin_t, gq))
    rope_k = _gained_rope_tables(cos_t, sin_t, _deinterleave(attn_k_norm[0], 1))
    small_logits = _logit_bound(attn_q_norm[0], attn_k_norm[0]) <= MAX_UNSHIFTED_LOGIT
    params = (norm_g, w_t, w_kg, rope_q, rope_k, small_logits, attn_w_out[0].astype(BF16),
              pool_w_in[0].astype(BF16), pool_w_group[0].astype(BF16),
              pool_scale[0].reshape(1, POOL_WIDTH), pool_w_out[0].astype(BF16))

    y_prompt = _trunk(x_prompt, mod[:, :, :bp], params)
    y_sample = _trunk(x_sample, mod[:, :, bp:bp + bs], params)
    return (y_prompt, y_sample)
```
